```python
import jax, jax.numpy as jnp
from jax import lax
import numpy as np

D_MODEL = 1024
BATCH = 8
SEQ = 4096
DEPTH = 2

GRID_W = 64
CTX_LEN = 256
EPS = 1e-6
ROPE_THETA = 10000.0
Q_BLOCK = 128
LRU_WIDTH = 512
LRU_BLOCKS = 8
LRU_BLOCK = LRU_WIDTH // LRU_BLOCKS
CONV_W = 4
LRU_C = 8.0
MLA_HEADS = 8
MLA_Q_RANK = 256
MLA_KV_RANK = 128
MLA_NOPE = 64
MLA_ROPE = 32
MLA_V = 64
MLA_SCALE = (MLA_NOPE + MLA_ROPE) ** -0.5
GQA_HEADS = 8
GQA_KV_HEADS = 2
GQA_GROUP = GQA_HEADS // GQA_KV_HEADS
GQA_DIM = 64
GQA_SCALE = GQA_DIM ** -0.5
N_BRANCH = 3
BRANCH_W = 512
N_GROUPS = 4
EXPERTS_PER_GROUP = 8
N_EXPERTS = N_GROUPS * EXPERTS_PER_GROUP
TOP_K = 2
D_EXPERT = 256
KV_SIZES = (LRU_WIDTH, MLA_KV_RANK, MLA_ROPE, GQA_KV_HEADS * GQA_DIM, GQA_KV_HEADS * GQA_DIM)
Q_SIZES = (LRU_WIDTH, MLA_Q_RANK, GQA_HEADS * GQA_DIM, N_BRANCH * D_MODEL)
KV_SIDE = LRU_WIDTH + MLA_KV_RANK + MLA_ROPE + 2 * GQA_KV_HEADS * GQA_DIM
D_IN = KV_SIDE + LRU_WIDTH + MLA_Q_RANK + GQA_HEADS * GQA_DIM + N_BRANCH * D_MODEL

kernel_name = "hybrid_flow_backbone"


def rms_norm(x, g):
    xf = x.astype(jnp.float32)
    y = xf * lax.rsqrt(jnp.mean(xf * xf, axis=-1, keepdims=True) + EPS)
    return (y * g.astype(jnp.float32)).astype(x.dtype)


def modulate(h, shift, scale):
    return h * (1 + scale) + shift


def split_cols(u, sizes):
    idx = [int(i) for i in np.cumsum(sizes)[:-1]]
    return jnp.split(u, idx, axis=-1)


def rope_tables(n, rot_dim):
    rows = n // GRID_W
    row = jnp.repeat(jnp.arange(rows), GRID_W).astype(jnp.float32)
    col = jnp.tile(jnp.arange(GRID_W), rows).astype(jnp.float32)
    quarter = rot_dim // 4
    freqs = ROPE_THETA ** (-jnp.arange(quarter, dtype=jnp.float32) / quarter)
    ang = jnp.concatenate([row[:, None] * freqs, col[:, None] * freqs], axis=-1)
    return jnp.cos(ang), jnp.sin(ang)


def apply_rope(x, cos, sin):
    xf = x.astype(jnp.float32)
    half = x.shape[-1] // 2
    x1, x2 = xf[..., :half], xf[..., half:]
    return jnp.concatenate([x1 * cos - x2 * sin, x1 * sin + x2 * cos], axis=-1).astype(x.dtype)


def attend(q, k, v, scale):
    s = jnp.einsum('bhgqd,bhkd->bhgqk', q, k, preferred_element_type=jnp.float32) * scale
    p = jax.nn.softmax(s, axis=-1).astype(v.dtype)
    return jnp.einsum('bhgqk,bhkd->bhgqd', p, v)


def blocked_attend(q, k, v, scale):
    b, h, g, n, d = q.shape
    qb = jnp.moveaxis(q.reshape(b, h, g, n // Q_BLOCK, Q_BLOCK, d), 3, 0)
    ob = lax.map(lambda qi: attend(qi, k, v, scale), qb)
    return jnp.moveaxis(ob, 0, 3).reshape(b, h, g, n, v.shape[-1])


def merge_heads(o):
    b, h, g, n, d = o.shape
    return o.transpose(0, 3, 1, 2, 4).reshape(b, n, h * g * d)


def dwconv(x, w, bias):
    pad = (CONV_W // 2, CONV_W - 1 - CONV_W // 2)
    y = lax.conv_general_dilated(x, w[:, None, :], window_strides=(1,), padding=[pad],
                                 dimension_numbers=('NWC', 'WIO', 'NWC'),
                                 feature_group_count=x.shape[-1])
    return y + bias


def lru_coeffs(xc, wa, ba, wi, bi, lam):
    b, s, _ = xc.shape
    xb = xc.reshape(b, s, LRU_BLOCKS, LRU_BLOCK)
    r = jax.nn.sigmoid(jnp.einsum('bsnc,ncd->bsnd', xb, wa.astype(jnp.float32)).reshape(b, s, -1) + ba)
    i = jax.nn.sigmoid(jnp.einsum('bsnc,ncd->bsnd', xb, wi.astype(jnp.float32)).reshape(b, s, -1) + bi)
    log_a = -LRU_C * r * jax.nn.softplus(-lam.astype(jnp.float32))
    a = jnp.exp(log_a)
    mult = jnp.sqrt(-jnp.expm1(2.0 * log_a))
    return a, mult * i * xc


def linear_scan(a, b, h0, reverse):
    if h0 is not None:
        idx = -1 if reverse else 0
        b = b.at[:, idx].add(a[:, idx] * h0)

    def combine(l, r):
        al, bl = l
        ar, br = r
        return al * ar, ar * bl + br

    _, h = lax.associative_scan(combine, (a, b), reverse=reverse, axis=1)
    return h


def rglru(xr_lat, xr_ctx, conv_w, conv_b, wa, ba, wi, bi, lam, need_ctx):
    xl = dwconv(xr_lat, conv_w, conv_b).astype(jnp.float32)
    xc = dwconv(xr_ctx, conv_w, conv_b).astype(jnp.float32)
    y_lat = jnp.zeros_like(xl)
    y_ctx = jnp.zeros_like(xc) if need_ctx else None
    for d, rev in enumerate((False, True)):
        a_c, b_c = lru_coeffs(xc, wa[d], ba[d], wi[d], bi[d], lam[d])
        h_c = linear_scan(a_c, b_c, None, rev)
        h_final = h_c[:, 0] if rev else h_c[:, -1]
        a_l, b_l = lru_coeffs(xl, wa[d], ba[d], wi[d], bi[d], lam[d])
        y_lat = y_lat + linear_scan(a_l, b_l, h_final, rev)
        if need_ctx:
            y_ctx = y_ctx + h_c
    y_lat = y_lat.astype(xr_lat.dtype)
    return y_lat, (y_ctx.astype(xr_ctx.dtype) if need_ctx else None)


def mla_kv(ckv, kr, g_kv, w_ukv, cos, sin):
    b, n, _ = ckv.shape
    kv = (rms_norm(ckv, g_kv) @ w_ukv).reshape(b, n, MLA_HEADS, MLA_NOPE + MLA_V).transpose(0, 2, 1, 3)
    k_nope, v = kv[..., :MLA_NOPE], kv[..., MLA_NOPE:]
    kr = kr[:, None]
    if cos is not None:
        kr = apply_rope(kr, cos, sin)
    k = jnp.concatenate([k_nope, jnp.broadcast_to(kr, (b, MLA_HEADS, n, MLA_ROPE))], axis=-1)
    return k, v


def mla_q(cq, g_q, w_uq, cos, sin):
    b, n, _ = cq.shape
    q = (rms_norm(cq, g_q) @ w_uq).reshape(b, n, MLA_HEADS, MLA_NOPE + MLA_ROPE).transpose(0, 2, 1, 3)
    qn, qr = q[..., :MLA_NOPE], q[..., MLA_NOPE:]
    if cos is not None:
        qr = apply_rope(qr, cos, sin)
    return jnp.concatenate([qn, qr], axis=-1)[:, :, None]


def gqa_heads(t, n_heads, g, cos, sin):
    b, n, _ = t.shape
    t = t.reshape(b, n, n_heads, GQA_DIM).transpose(0, 2, 1, 3)
    if g is not None:
        t = rms_norm(t, g)
    if cos is not None:
        t = apply_rope(t, cos, sin)
    return t


def merge_branches(y_rnn, y_mla, y_gqa, mg, w_branch, w_out):
    br = jnp.stack([y_rnn, y_mla, y_gqa], axis=-2)
    proj = jnp.einsum('bnkc,kcd->bnkd', br, w_branch)
    gates = jax.nn.sigmoid(mg.reshape(mg.shape[:-1] + (N_BRANCH, D_MODEL)))
    return jnp.sum(gates * proj, axis=-2) @ w_out


def mixer(x, ctx, sh, sc, csh, csc, g_mix, w_in, conv_w, conv_b, lru_wa, lru_ba, lru_wi, lru_bi,
          lru_lambda, mla_gq, mla_wuq, mla_gkv, mla_wukv, gqa_gq, gqa_gk, w_branch, w_out,
          ropes, update_ctx):
    cos_m, sin_m, cos_g, sin_g = ropes
    h = modulate(rms_norm(x, g_mix), sh[:, None], sc[:, None])
    hc = modulate(rms_norm(ctx, g_mix), csh, csc)
    xr, ckv, kr, gk, gv, rg, cq, gq, mg = split_cols(h @ w_in, KV_SIZES + Q_SIZES)
    if update_ctx:
        xr_c, ckv_c, kr_c, gk_c, gv_c, rg_c, cq_c, gq_c, mg_c = split_cols(hc @ w_in, KV_SIZES + Q_SIZES)
    else:
        xr_c, ckv_c, kr_c, gk_c, gv_c = split_cols(hc @ w_in[:, :KV_SIDE], KV_SIZES)

    lru_l, lru_c = rglru(xr, xr_c, conv_w, conv_b, lru_wa, lru_ba, lru_wi, lru_bi, lru_lambda, update_ctx)
    y_rnn = lru_l * jax.nn.gelu(rg)

    k_c, v_c = mla_kv(ckv_c, kr_c, mla_gkv, mla_wukv, None, None)
    k_l, v_l = mla_kv(ckv, kr, mla_gkv, mla_wukv, cos_m, sin_m)
    q_l = mla_q(cq, mla_gq, mla_wuq, cos_m, sin_m)
    y_mla = merge_heads(blocked_attend(q_l, jnp.concatenate([k_c, k_l], axis=2),
                                       jnp.concatenate([v_c, v_l], axis=2), MLA_SCALE))

    b, n, _ = x.shape
    gk_ch = gqa_heads(gk_c, GQA_KV_HEADS, gqa_gk, None, None)
    gv_ch = gqa_heads(gv_c, GQA_KV_HEADS, None, None, None)
    gk_lh = gqa_heads(gk, GQA_KV_HEADS, gqa_gk, cos_g, sin_g)
    gv_lh = gqa_heads(gv, GQA_KV_HEADS, None, None, None)
    gq_lh = gqa_heads(gq, GQA_HEADS, gqa_gq, cos_g, sin_g).reshape(b, GQA_KV_HEADS, GQA_GROUP, n, GQA_DIM)
    y_gqa = merge_heads(blocked_attend(gq_lh, jnp.concatenate([gk_ch, gk_lh], axis=2),
                                       jnp.concatenate([gv_ch, gv_lh], axis=2), GQA_SCALE))

    out = merge_branches(y_rnn, y_mla, y_gqa, mg, w_branch, w_out)
    if not update_ctx:
        return out, None

    nc = ctx.shape[1]
    yc_rnn = lru_c * jax.nn.gelu(rg_c)
    qc = mla_q(cq_c, mla_gq, mla_wuq, None, None)
    yc_mla = merge_heads(attend(qc, k_c, v_c, MLA_SCALE))
    gq_ch = gqa_heads(gq_c, GQA_HEADS, gqa_gq, None, None).reshape(b, GQA_KV_HEADS, GQA_GROUP, nc, GQA_DIM)
    yc_gqa = merge_heads(attend(gq_ch, gk_ch, gv_ch, GQA_SCALE))
    out_c = merge_branches(yc_rnn, yc_mla, yc_gqa, mg_c, w_branch, w_out)
    return out, out_c


def hier_moe(h, wg, bg, we, be, w1, w3, w2):
    b, n, d = h.shape
    t = h.reshape(-1, d)
    glog = (t @ wg + bg).astype(jnp.float32)
    gprob = jax.nn.softmax(glog, axis=-1)
    _, gsel = lax.top_k(glog, 1)
    pg = jnp.take_along_axis(gprob, gsel, axis=-1)
    elog = (t @ we + be).astype(jnp.float32).reshape(-1, N_GROUPS, EXPERTS_PER_GROUP)
    idx = jnp.broadcast_to(gsel[:, :, None], (t.shape[0], 1, EXPERTS_PER_GROUP))
    elog = jnp.take_along_axis(elog, idx, axis=1)[:, 0]
    top_v, top_i = lax.top_k(elog, TOP_K)
    pe = jax.nn.softmax(top_v, axis=-1) * pg
    eidx = gsel * EXPERTS_PER_GROUP + top_i
    comb = jnp.sum(jax.nn.one_hot(eidx, N_EXPERTS, dtype=jnp.float32) * pe[..., None], axis=1).astype(t.dtype)
    out = jnp.zeros_like(t)
    for e in range(N_EXPERTS):
        he = jax.nn.silu(t @ w1[e]) * (t @ w3[e])
        out = out + comb[:, e:e + 1] * (he @ w2[e])
    return out.reshape(b, n, d)


def setup_inputs(seed: int = 0) -> dict:
    key = jax.random.key(seed)
    ks = iter(jax.random.split(key, 40))
    L, D = DEPTH, D_MODEL

    def nrm(shape, scale):
        return jax.random.normal(next(ks), shape, jnp.float32) * scale

    def gain(shape):
        return 1.0 + nrm(shape, 0.02)

    u = jax.random.uniform(next(ks), (L, 2, LRU_WIDTH), jnp.float32, minval=0.9, maxval=0.999)
    s = u ** (1.0 / LRU_C)
    lam = jnp.log(s) - jnp.log1p(-s)
    return {
        "x": nrm((BATCH, SEQ, D), 1.0),
        "c": nrm((BATCH, D), 1.0),
        "ctx": nrm((BATCH, CTX_LEN, D), 1.0),
        "c_ctx": nrm((D,), 1.0),
        "w_mod": nrm((L, D, 6 * D), 0.5 * D ** -0.5),
        "b_mod": nrm((L, 6 * D), 0.01),
        "g_mix": gain((L, D)),
        "g_ffn": gain((L, D)),
        "w_in": nrm((L, D, D_IN), D ** -0.5),
        "conv_w": nrm((L, CONV_W, LRU_WIDTH), CONV_W ** -0.5),
        "conv_b": nrm((L, LRU_WIDTH), 0.01),
        "lru_wa": nrm((L, 2, LRU_BLOCKS, LRU_BLOCK, LRU_BLOCK), LRU_BLOCK ** -0.5),
        "lru_ba": nrm((L, 2, LRU_WIDTH), 0.01),
        "lru_wi": nrm((L, 2, LRU_BLOCKS, LRU_BLOCK, LRU_BLOCK), LRU_BLOCK ** -0.5),
        "lru_bi": nrm((L, 2, LRU_WIDTH), 0.01),
        "lru_lambda": lam,
        "mla_gq": gain((L, MLA_Q_RANK)),
        "mla_wuq": nrm((L, MLA_Q_RANK, MLA_HEADS * (MLA_NOPE + MLA_ROPE)), MLA_Q_RANK ** -0.5),
        "mla_gkv": gain((L, MLA_KV_RANK)),
        "mla_wukv": nrm((L, MLA_KV_RANK, MLA_HEADS * (MLA_NOPE + MLA_V)), MLA_KV_RANK ** -0.5),
        "gqa_gq": gain((L, GQA_DIM)),
        "gqa_gk": gain((L, GQA_DIM)),
        "w_branch": nrm((L, N_BRANCH, BRANCH_W, D), BRANCH_W ** -0.5),
        "w_out": nrm((L, D, D), D ** -0.5),
        "moe_wg": nrm((L, D, N_GROUPS), D ** -0.5),
        "moe_bg": nrm((L, N_GROUPS), 0.01),
        "moe_we": nrm((L, D, N_EXPERTS), D ** -0.5),
        "moe_be": nrm((L, N_EXPERTS), 0.01),
        "moe_w1": nrm((L, N_EXPERTS, D, D_EXPERT), D ** -0.5),
        "moe_w3": nrm((L, N_EXPERTS, D, D_EXPERT), D ** -0.5),
        "moe_w2": nrm((L, N_EXPERTS, D_EXPERT, D), D_EXPERT ** -0.5),
        "g_final": gain((D,)),
    }


def reference(x, c, ctx, c_ctx, w_mod, b_mod, g_mix, g_ffn, w_in, conv_w, conv_b, lru_wa, lru_ba,
              lru_wi, lru_bi, lru_lambda, mla_gq, mla_wuq, mla_gkv, mla_wukv, gqa_gq, gqa_gk,
              w_branch, w_out, moe_wg, moe_bg, moe_we, moe_be, moe_w1, moe_w3, moe_w2, g_final):
    n = x.shape[1]
    nc = ctx.shape[1]
    ropes = rope_tables(n, MLA_ROPE) + rope_tables(n, GQA_DIM)
    for l in range(DEPTH):
        update_ctx = l < DEPTH - 1
        mod = jax.nn.silu(c) @ w_mod[l] + b_mod[l]
        mod_c = jax.nn.silu(c_ctx) @ w_mod[l] + b_mod[l]
        sh_a, sc_a, g_a, sh_f, sc_f, g_f = jnp.split(mod, 6, axis=-1)
        csh_a, csc_a, cg_a, csh_f, csc_f, cg_f = jnp.split(mod_c, 6, axis=-1)
        out, out_c = mixer(x, ctx, sh_a, sc_a, csh_a, csc_a, g_mix[l], w_in[l], conv_w[l], conv_b[l],
                           lru_wa[l], lru_ba[l], lru_wi[l], lru_bi[l], lru_lambda[l], mla_gq[l],
                           mla_wuq[l], mla_gkv[l], mla_wukv[l], gqa_gq[l], gqa_gk[l], w_branch[l],
                           w_out[l], ropes, update_ctx)
        x = x + g_a[:, None] * out
        h = modulate(rms_norm(x, g_ffn[l]), sh_f[:, None], sc_f[:, None])
        if update_ctx:
            ctx = ctx + cg_a * out_c
            hc = modulate(rms_norm(ctx, g_ffn[l]), csh_f, csc_f)
            y = hier_moe(jnp.concatenate([hc, h], axis=1), moe_wg[l], moe_bg[l], moe_we[l], moe_be[l],
                         moe_w1[l], moe_w3[l], moe_w2[l])
            ctx = ctx + cg_f * y[:, :nc]
            x = x + g_f[:, None] * y[:, nc:]
        else:
            x = x + g_f[:, None] * hier_moe(h, moe_wg[l], moe_bg[l], moe_we[l], moe_be[l],
                                            moe_w1[l], moe_w3[l], moe_w2[l])
    return rms_norm(x, g_final)
```

```python
import functools

import numpy as np
import jax
import jax.numpy as jnp
from jax import lax
from jax.experimental import pallas as pl
from jax.experimental.pallas import tpu as pltpu

F32 = jnp.float32
BF16 = jnp.bfloat16

EPS = 1e-6
ROPE_THETA = 10000.0
GRID_W = 64
LOG2E = 1.4426950408889634

LRU_WIDTH = 512
LRU_BLOCKS = 8
LRU_BLOCK = LRU_WIDTH // LRU_BLOCKS
CONV_W = 4
LRU_C = 8.0
MLA_HEADS = 8
MLA_Q_RANK = 256
MLA_KV_RANK = 128
MLA_NOPE = 64
MLA_ROPE = 32
MLA_V = 64
MLA_SCALE = (MLA_NOPE + MLA_ROPE) ** -0.5
GQA_HEADS = 8
GQA_KV_HEADS = 2
GQA_GROUP = GQA_HEADS // GQA_KV_HEADS
GQA_DIM = 64
GQA_SCALE = GQA_DIM ** -0.5
N_BRANCH = 3
BRANCH_W = 512
N_GROUPS = 4
EXPERTS_PER_GROUP = 8
N_EXPERTS = N_GROUPS * EXPERTS_PER_GROUP
D_EXPERT = 256

LANES = 128
TM = 256
KC = 512
VMEM_LIMIT = 56 * 1024 * 1024

C_XR = 0
C_RG = C_XR + LRU_WIDTH
C_MG = C_RG + LRU_WIDTH
C_GQ = C_MG + N_BRANCH * 1024
C_SMALL = C_GQ + GQA_HEADS * GQA_DIM
S_CQ = 0
S_CKV = S_CQ + MLA_Q_RANK
S_KR = S_CKV + MLA_KV_RANK
S_GK = S_KR + LANES
S_GV = S_GK + GQA_KV_HEADS * GQA_DIM
SMALL_W = S_GV + GQA_KV_HEADS * GQA_DIM
MLA_SLOT = 128
ROPE_LANE0 = MLA_NOPE


def _cparams(n_axes):
    return pltpu.CompilerParams(dimension_semantics=("arbitrary",) * n_axes,
                                vmem_limit_bytes=VMEM_LIMIT)


def _rms(x, g):
    return x * lax.rsqrt(jnp.mean(x * x, axis=-1, keepdims=True) + EPS) * g


def _sigmoid(x):
    return 1.0 / (1.0 + jnp.exp(-x))


def _lane_iota(shape):
    return lax.broadcasted_iota(jnp.int32, shape, len(shape) - 1)


def _mod_kernel(c_ref, w_ref, b_ref, o_ref):
    c = c_ref[...]
    s = c * _sigmoid(c)
    o_ref[...] = jnp.dot(s, w_ref[...], preferred_element_type=F32) + b_ref[...]


def _mod_call(cc, w, b):
    rows, d = cc.shape
    n = w.shape[1]
    bn = 512
    return pl.pallas_call(
        _mod_kernel,
        grid=(n // bn,),
        in_specs=[pl.BlockSpec((rows, d), lambda j: (0, 0)),
                  pl.BlockSpec((d, bn), lambda j: (0, j)),
                  pl.BlockSpec((1, bn), lambda j: (0, j))],
        out_specs=pl.BlockSpec((rows, bn), lambda j: (0, j)),
        out_shape=jax.ShapeDtypeStruct((rows, n), F32),
        compiler_params=_cparams(1),
        name="adaln_mod",
    )(cc, w, b.reshape(1, n))


def _rope_tables(n_ctx, seq):
    rows = seq // GRID_W
    row = np.repeat(np.arange(rows), GRID_W).astype(np.float64)
    col = np.tile(np.arange(GRID_W), rows).astype(np.float64)

    def angles(rot_dim):
        quarter = rot_dim // 4
        freqs = ROPE_THETA ** (-np.arange(quarter, dtype=np.float64) / quarter)
        return np.concatenate([row[:, None] * freqs, col[:, None] * freqs], axis=-1)

    n = n_ctx + seq
    am = angles(MLA_ROPE)
    half = MLA_ROPE // 2
    cm = np.ones((n, LANES)); sm = np.zeros((n, LANES))
    cm[n_ctx:, ROPE_LANE0:ROPE_LANE0 + half] = np.cos(am)
    cm[n_ctx:, ROPE_LANE0 + half:ROPE_LANE0 + 2 * half] = np.cos(am)
    sm[n_ctx:, ROPE_LANE0:ROPE_LANE0 + half] = -np.sin(am)
    sm[n_ctx:, ROPE_LANE0 + half:ROPE_LANE0 + 2 * half] = np.sin(am)
    ag = angles(GQA_DIM)
    hg = GQA_DIM // 2
    cg = np.ones((n, LANES)); sg = np.zeros((n, LANES))
    for h0 in (0, GQA_DIM):
        cg[n_ctx:, h0:h0 + hg] = np.cos(ag)
        cg[n_ctx:, h0 + hg:h0 + 2 * hg] = np.cos(ag)
        sg[n_ctx:, h0:h0 + hg] = -np.sin(ag)
        sg[n_ctx:, h0 + hg:h0 + 2 * hg] = np.sin(ag)
    tab_m = np.concatenate([cm, sm], axis=-1).astype(np.float32)
    tab_g = np.concatenate([cg, sg], axis=-1).astype(np.float32)
    return jnp.asarray(tab_m), jnp.asarray(tab_g)


def _rope_mla(y, cos, sin):
    half = MLA_ROPE // 2
    lane = _lane_iota(y.shape)
    rot = jnp.where(lane < ROPE_LANE0 + half,
                    pltpu.roll(y, LANES - half, 1),
                    pltpu.roll(y, half, 1))
    return y * cos + rot * sin


def _rope_gqa(y, cos, sin):
    half = GQA_DIM // 2
    lane = _lane_iota(y.shape)
    rot = jnp.where((lane % GQA_DIM) < half,
                    pltpu.roll(y, LANES - half, 1),
                    pltpu.roll(y, half, 1))
    return y * cos + rot * sin


def _head_rms_pair(t, g):
    lane = _lane_iota(t.shape)
    lo = lane < GQA_DIM
    t2 = t * t
    s_lo = jnp.sum(jnp.where(lo, t2, 0.0), axis=-1, keepdims=True)
    s_hi = jnp.sum(jnp.where(lo, 0.0, t2), axis=-1, keepdims=True)
    r = jnp.where(lo, lax.rsqrt(s_lo * (1.0 / GQA_DIM) + EPS), lax.rsqrt(s_hi * (1.0 / GQA_DIM) + EPS))
    return t * r * g


def _inproj_kernel(x_ref, mod_ref, gmix_ref, w_ref, wuq_ref, wukv_ref, place_ref,
                   gq_ref, gkv_ref, ggq_ref, ggk_ref, tabm_ref, tabg_ref,
                   xr_ref, rg_ref, mg_ref, qm_ref, km_ref, vm_ref, qg_ref, kg_ref, vg_ref):
    x = x_ref[0]
    m6 = mod_ref[0, 0]
    sh, sc = m6[0:1], m6[1:2]
    h = (_rms(x, gmix_ref[...]) * (1.0 + sc) + sh).astype(BF16)

    def proj(c0, width):
        return jnp.dot(h, w_ref[:, c0:c0 + width], preferred_element_type=F32)

    xr_ref[0] = proj(C_XR, LRU_WIDTH)
    rg_ref[0] = proj(C_RG, LRU_WIDTH).astype(BF16)
    for j in range(N_BRANCH * 1024 // 512):
        mg_ref[0, :, j * 512:(j + 1) * 512] = proj(C_MG + j * 512, 512).astype(BF16)

    cos_m, sin_m = tabm_ref[:, :LANES], tabm_ref[:, LANES:]
    cos_g, sin_g = tabg_ref[:, :LANES], tabg_ref[:, LANES:]

    gq = proj(C_GQ, GQA_HEADS * GQA_DIM)
    for v in range(GQA_HEADS * GQA_DIM // LANES):
        t = gq[:, v * LANES:(v + 1) * LANES]
        y = _rope_gqa(_head_rms_pair(t, ggq_ref[...]), cos_g, sin_g)
        qg_ref[0, :, v * LANES:(v + 1) * LANES] = (y * (GQA_SCALE * LOG2E)).astype(BF16)

    small = proj(C_SMALL, SMALL_W)
    cq = small[:, S_CQ:S_CQ + MLA_Q_RANK]
    ckv = small[:, S_CKV:S_CKV + MLA_KV_RANK]
    krp = small[:, S_KR:S_KR + LANES]
    gk = small[:, S_GK:S_GK + LANES]
    gv = small[:, S_GV:S_GV + LANES]

    kg_ref[0] = _rope_gqa(_head_rms_pair(gk, ggk_ref[...]), cos_g, sin_g).astype(BF16)
    vg_ref[0] = gv.astype(BF16)

    cqn = _rms(cq, gq_ref[...]).astype(BF16)
    qu = jnp.dot(cqn, wuq_ref[...], preferred_element_type=F32)
    for hh in range(MLA_HEADS):
        y = _rope_mla(qu[:, hh * MLA_SLOT:(hh + 1) * MLA_SLOT], cos_m, sin_m)
        qm_ref[0, :, hh * MLA_SLOT:(hh + 1) * MLA_SLOT] = (y * (MLA_SCALE * LOG2E)).astype(BF16)

    ckvn = _rms(ckv, gkv_ref[...]).astype(BF16)
    kvu = jnp.dot(ckvn, wukv_ref[...], preferred_element_type=F32)
    kr = _rope_mla(krp, cos_m, sin_m).astype(BF16)
    kr_placed = jnp.dot(kr, place_ref[...], preferred_element_type=F32)
    n_k = MLA_HEADS * MLA_SLOT
    km_ref[0] = (kvu[:, :n_k] + kr_placed).astype(BF16)
    vm_ref[0] = kvu[:, n_k:].astype(BF16)


def _inproj_call(xx, mod6, gmix, w_all, wuq, wukv, place, gq, gkv, ggq, ggk, tab_m, tab_g, n_ctx_tiles):
    b, n, d = xx.shape
    nt = n // TM
    ncol = w_all.shape[1]

    def tok(width):
        return pl.BlockSpec((1, TM, width), lambda bi, i: (bi, i, 0))

    def const(shape):
        return pl.BlockSpec(shape, lambda bi, i: (0,) * len(shape))

    out_w = [(LRU_WIDTH, F32), (LRU_WIDTH, BF16), (N_BRANCH * 1024, BF16),
             (MLA_HEADS * MLA_SLOT, BF16), (MLA_HEADS * MLA_SLOT, BF16), (MLA_HEADS * MLA_V, BF16),
             (GQA_HEADS * GQA_DIM, BF16), (GQA_KV_HEADS * GQA_DIM, BF16), (GQA_KV_HEADS * GQA_DIM, BF16)]
    return pl.pallas_call(
        _inproj_kernel,
        grid=(b, nt),
        in_specs=[tok(d),
                  pl.BlockSpec((1, 1, 6, d), lambda bi, i: (bi, jnp.minimum(i // n_ctx_tiles, 1), 0, 0)),
                  const((1, d)), const((d, ncol)), const(wuq.shape), const(wukv.shape), const(place.shape),
                  const((1, MLA_Q_RANK)), const((1, MLA_KV_RANK)), const((1, LANES)), const((1, LANES)),
                  pl.BlockSpec((TM, 2 * LANES), lambda bi, i: (i, 0)),
                  pl.BlockSpec((TM, 2 * LANES), lambda bi, i: (i, 0))],
        out_specs=[tok(w) for w, _ in out_w],
        out_shape=[jax.ShapeDtypeStruct((b, n, w), dt) for w, dt in out_w],
        compiler_params=_cparams(2),
        name="in_proj",
    )(xx, mod6, gmix, w_all, wuq, wukv, place, gq, gkv, ggq, ggk, tab_m, tab_g)


def _online_softmax(q, k_ref, v_ref, kcol, kw, n_ctx, n_lat_chunks, acc_w):
    rows = q.shape[0]

    def step(start, size, carry):
        m, l, acc = carry
        k = k_ref[pl.ds(start, size), kcol:kcol + kw]
        v = v_ref[pl.ds(start, size), :]
        s = lax.dot_general(q, k, (((1,), (1,)), ((), ())), preferred_element_type=F32)
        m_new = jnp.maximum(m, jnp.max(s, axis=1, keepdims=True))
        alpha = jnp.exp2(m - m_new)
        p = jnp.exp2(s - m_new)
        l = alpha * l + jnp.sum(p, axis=1, keepdims=True)
        acc = alpha * acc + jnp.dot(p.astype(BF16), v, preferred_element_type=F32)
        return m_new, l, acc

    carry = (jnp.full((rows, 1), -jnp.inf, F32), jnp.zeros((rows, 1), F32), jnp.zeros((rows, acc_w), F32))
    carry = step(0, n_ctx, carry)

    def body(j, c):
        return step(pl.multiple_of(n_ctx + j * KC, TM), KC, c)

    _, l, acc = lax.fori_loop(0, n_lat_chunks, body, carry)
    return acc * (1.0 / l)


def _mla_attn_kernel(q_ref, k_ref, v_ref, o_ref, *, n_ctx, n_lat):
    i = pl.program_id(2)
    n_chunks = jnp.where(i < n_ctx // TM, 0, n_lat // KC)
    lane = _lane_iota((TM, 2 * MLA_V))
    out = None
    for hh in range(2):
        q = q_ref[0, :, hh * MLA_SLOT:(hh + 1) * MLA_SLOT]
        o = _online_softmax(q, k_ref.at[0], v_ref.at[0], hh * MLA_SLOT, MLA_SLOT, n_ctx, n_chunks, 2 * MLA_V)
        out = o if out is None else jnp.where(lane < MLA_V, out, o)
    o_ref[0] = out.astype(BF16)


def _mla_attn_call(qm, km, vm, n_ctx):
    b, n, _ = qm.shape
    nt = n // TM
    pairs = MLA_HEADS // 2
    kern = functools.partial(_mla_attn_kernel, n_ctx=n_ctx, n_lat=n - n_ctx)
    return pl.pallas_call(
        kern,
        grid=(b, pairs, nt),
        in_specs=[pl.BlockSpec((1, TM, 2 * MLA_SLOT), lambda bi, p, i: (bi, i, p)),
                  pl.BlockSpec((1, n, 2 * MLA_SLOT), lambda bi, p, i: (bi, 0, p)),
                  pl.BlockSpec((1, n, 2 * MLA_V), lambda bi, p, i: (bi, 0, p))],
        out_specs=pl.BlockSpec((1, TM, 2 * MLA_V), lambda bi, p, i: (bi, i, p)),
        out_shape=jax.ShapeDtypeStruct((b, n, MLA_HEADS * MLA_V), BF16),
        compiler_params=_cparams(3),
        name="mla_attention",
    )(qm, km, vm)


def _gqa_attn_kernel(q_ref, k_ref, v_ref, rep_ref, o_ref, krep, vrep, *, n_ctx, n_lat):
    i = pl.program_id(2)
    n = n_ctx + n_lat
    gw = GQA_GROUP * GQA_DIM

    @pl.when(i == 0)
    def _():
        for c in range(n // TM):
            rows = pl.ds(c * TM, TM)
            krep[rows, :] = jnp.dot(k_ref[0, rows, :], rep_ref[0], preferred_element_type=F32).astype(BF16)
            vrep[rows, :] = jnp.dot(v_ref[0, rows, :], rep_ref[0], preferred_element_type=F32).astype(BF16)

    n_chunks = jnp.where(i < n_ctx // TM, 0, n_lat // KC)
    q = q_ref[0]
    head = _lane_iota((TM, gw)) // GQA_DIM
    out = jnp.zeros((TM, gw), F32)
    for j in range(GQA_GROUP):
        qj = jnp.where(head == j, q, jnp.zeros_like(q))
        o = _online_softmax(qj, krep, vrep, 0, gw, n_ctx, n_chunks, gw)
        out = jnp.where(head == j, o, out)
    o_ref[0] = out.astype(BF16)


def _gqa_attn_call(qg, kg, vg, rep, n_ctx):
    b, n, _ = qg.shape
    nt = n // TM
    gw = GQA_GROUP * GQA_DIM
    kvw = GQA_KV_HEADS * GQA_DIM
    kern = functools.partial(_gqa_attn_kernel, n_ctx=n_ctx, n_lat=n - n_ctx)
    return pl.pallas_call(
        kern,
        grid=(b, GQA_KV_HEADS, nt),
        in_specs=[pl.BlockSpec((1, TM, gw), lambda bi, g, i: (bi, i, g)),
                  pl.BlockSpec((1, n, kvw), lambda bi, g, i: (bi, 0, 0)),
                  pl.BlockSpec((1, n, kvw), lambda bi, g, i: (bi, 0, 0)),
                  pl.BlockSpec((1, kvw, gw), lambda bi, g, i: (g, 0, 0))],
        out_specs=pl.BlockSpec((1, TM, gw), lambda bi, g, i: (bi, i, g)),
        out_shape=jax.ShapeDtypeStruct((b, n, GQA_HEADS * GQA_DIM), BF16),
        scratch_shapes=[pltpu.VMEM((n, gw), BF16), pltpu.VMEM((n, gw), BF16)],
        compiler_params=_cparams(3),
        name="gqa_attention",
    )(qg, kg, vg, rep)


def _softplus(z):
    return jnp.maximum(z, 0.0) + jnp.log1p(jnp.exp(-jnp.abs(z)))


def _lru_kernel(*refs, reverse, n_chunks, n_ctx_chunks):
    if reverse:
        (x_ref, hp_ref, hn_ref, cw_ref, cb_ref, wa_ref, wi_ref, ba_ref, bi_ref, lam_ref, yin_ref,
         y_ref, xs, a_s, b_s, h_s) = refs
    else:
        (x_ref, hp_ref, hn_ref, cw_ref, cb_ref, wa_ref, wi_ref, ba_ref, bi_ref, lam_ref,
         y_ref, xs, a_s, b_s, h_s) = refs
        yin_ref = None
    j = pl.program_id(1)
    c = jnp.where(j == 0, 0, n_chunks - j) if reverse else j
    ch, bsz, lw = x_ref.shape

    @pl.when(j == 0)
    def _():
        h_s[...] = jnp.zeros_like(h_s)

    seq_first = jnp.logical_or(c == 0, c == n_ctx_chunks)
    seq_last = jnp.logical_or(c == n_ctx_chunks - 1, c == n_chunks - 1)
    xs[0:2] = jnp.where(seq_first, 0.0, hp_ref[...])
    xs[2:2 + ch] = x_ref[...]
    xs[2 + ch:3 + ch] = jnp.where(seq_last, 0.0, hn_ref[...])
    xc = cb_ref[...]
    for tap in range(CONV_W):
        xc = xc + cw_ref[tap:tap + 1, :] * xs[tap:tap + ch]
    x2 = xc.reshape(ch * bsz, lw)
    r = _sigmoid(jnp.dot(x2, wa_ref[0], preferred_element_type=F32) + ba_ref[...])
    gi = _sigmoid(jnp.dot(x2, wi_ref[0], preferred_element_type=F32) + bi_ref[...])
    log_a = (-LRU_C) * r * _softplus(-lam_ref[...])
    a = jnp.exp(log_a)
    a_s[...] = a.reshape(ch, bsz, lw)
    mult = jnp.sqrt(-jnp.tanh(log_a) * (a * a + 1.0))
    b_s[...] = (mult * gi * x2).reshape(ch, bsz, lw)

    def body(tt, h):
        t = ch - 1 - tt if reverse else tt
        h = a_s[t] * h + b_s[t]
        y_ref[t] = h + yin_ref[t] if reverse else h
        return h

    h_s[...] = lax.fori_loop(0, ch, body, h_s[...], unroll=8)


def _lru_call(xr_t, yin, cw, cb, wa_bd, wi_bd, ba, bi, lam, n_ctx, reverse):
    n, bsz, width = xr_t.shape
    ch = TM
    n_chunks = n // ch
    n_ctx_chunks = n_ctx // ch
    groups = width // LANES

    def cmap(j):
        return jnp.where(j == 0, 0, n_chunks - j) if reverse else j

    main = pl.BlockSpec((ch, bsz, LANES), lambda g, j: (cmap(j), 0, g))
    in_specs = [main,
                pl.BlockSpec((2, bsz, LANES), lambda g, j: (jnp.maximum(cmap(j) * (ch // 2) - 1, 0), 0, g)),
                pl.BlockSpec((1, bsz, LANES), lambda g, j: (jnp.minimum((cmap(j) + 1) * ch, n - 1), 0, g)),
                pl.BlockSpec((CONV_W, LANES), lambda g, j: (0, g)),
                pl.BlockSpec((1, LANES), lambda g, j: (0, g)),
                pl.BlockSpec((1, LANES, LANES), lambda g, j: (g, 0, 0)),
                pl.BlockSpec((1, LANES, LANES), lambda g, j: (g, 0, 0)),
                pl.BlockSpec((1, LANES), lambda g, j: (0, g)),
                pl.BlockSpec((1, LANES), lambda g, j: (0, g)),
                pl.BlockSpec((1, LANES), lambda g, j: (0, g))]
    args = [xr_t, xr_t, xr_t, cw, cb, wa_bd, wi_bd, ba, bi, lam]
    if reverse:
        in_specs.append(main)
        args.append(yin)
    kern = functools.partial(_lru_kernel, reverse=reverse, n_chunks=n_chunks, n_ctx_chunks=n_ctx_chunks)
    return pl.pallas_call(
        kern,
        grid=(groups, n_chunks),
        in_specs=in_specs,
        out_specs=main,
        out_shape=jax.ShapeDtypeStruct((n, bsz, width), F32),
        scratch_shapes=[pltpu.VMEM((ch + 3, bsz, LANES), F32), pltpu.VMEM((ch, bsz, LANES), F32),
                        pltpu.VMEM((ch, bsz, LANES), F32), pltpu.VMEM((bsz, LANES), F32)],
        compiler_params=_cparams(2),
        name="rglru_rev" if reverse else "rglru_fwd",
    )(*args)


def _gelu_tanh(x):
    return 0.5 * x * (1.0 + jnp.tanh(0.7978845608028654 * (x + 0.044715 * (x * x * x))))


def _merge_kernel(x_ref, mod_ref, ylru_ref, rg_ref, ym_ref, yg_ref, mg_ref, wbr_ref, wout_ref,
                  gffn_ref, wr_ref, br_ref, x1_ref, hf_ref, ri_ref):
    d = x_ref.shape[-1]
    m6 = mod_ref[0, 0]
    g_a, sh_f, sc_f = m6[2:3], m6[3:4], m6[4:5]
    y_rnn = (ylru_ref[0] * _gelu_tanh(rg_ref[0].astype(F32))).astype(BF16)
    acc = jnp.zeros((TM, d), F32)
    for kbr, br in enumerate((y_rnn, ym_ref[0], yg_ref[0])):
        pr = jnp.dot(br, wbr_ref[kbr], preferred_element_type=F32)
        acc = acc + _sigmoid(mg_ref[0, :, kbr * d:(kbr + 1) * d].astype(F32)) * pr
    out = jnp.dot(acc.astype(BF16), wout_ref[...], preferred_element_type=F32)
    x1 = x_ref[0] + g_a * out
    x1_ref[0] = x1
    hf = _rms(x1, gffn_ref[...]) * (1.0 + sc_f) + sh_f
    hf_ref[0] = hf

    lg = jnp.dot(hf, wr_ref[...], preferred_element_type=F32, precision=lax.Precision.HIGHEST) + br_ref[...]
    lane = _lane_iota(lg.shape)
    big = jnp.int32(1 << 20)
    neg = -jnp.inf
    is_g = lane < N_GROUPS
    gl = jnp.where(is_g, lg, neg)
    gmax = jnp.max(gl, axis=-1, keepdims=True)
    gsel = jnp.min(jnp.where(gl == gmax, lane, big), axis=-1, keepdims=True)
    pg = 1.0 / jnp.sum(jnp.where(is_g, jnp.exp(lg - gmax), 0.0), axis=-1, keepdims=True)
    e0 = N_GROUPS + gsel * EXPERTS_PER_GROUP
    el = jnp.where(jnp.logical_and(lane >= e0, lane < e0 + EXPERTS_PER_GROUP), lg, neg)
    v1 = jnp.max(el, axis=-1, keepdims=True)
    i1 = jnp.min(jnp.where(el == v1, lane, big), axis=-1, keepdims=True)
    el2 = jnp.where(lane == i1, neg, el)
    v2 = jnp.max(el2, axis=-1, keepdims=True)
    i2 = jnp.min(jnp.where(el2 == v2, lane, big), axis=-1, keepdims=True)
    t = jnp.exp(v2 - v1)
    p1 = pg / (1.0 + t)
    p2 = p1 * t
    ri = jnp.where(lane == 0, (i1 - N_GROUPS).astype(F32),
                   jnp.where(lane == 1, (i2 - N_GROUPS).astype(F32),
                             jnp.where(lane == 2, p1, jnp.where(lane == 3, p2, 0.0))))
    ri_ref[0] = ri


def _merge_call(xx, mod6, ylru, rg, ym, yg, mg, wbr, wout, gffn, wr, br, n_ctx_tiles):
    b, n, d = xx.shape
    nt = n // TM

    def tok(width):
        return pl.BlockSpec((1, TM, width), lambda bi, i: (bi, i, 0))

    def const(shape):
        return pl.BlockSpec(shape, lambda bi, i: (0,) * len(shape))

    return pl.pallas_call(
        _merge_kernel,
        grid=(b, nt),
        in_specs=[tok(d),
                  pl.BlockSpec((1, 1, 6, d), lambda bi, i: (bi, jnp.minimum(i // n_ctx_tiles, 1), 0, 0)),
                  tok(BRANCH_W), tok(BRANCH_W), tok(BRANCH_W), tok(BRANCH_W), tok(N_BRANCH * d),
                  const(wbr.shape), const(wout.shape), const((1, d)), const(wr.shape), const((1, LANES))],
        out_specs=[tok(d), tok(d), tok(LANES)],
        out_shape=[jax.ShapeDtypeStruct((b, n, d), F32), jax.ShapeDtypeStruct((b, n, d), F32),
                   jax.ShapeDtypeStruct((b, n, LANES), F32)],
        compiler_params=_cparams(2),
        name="merge_router",
    )(xx, mod6, ylru, rg, ym, yg, mg, wbr, wout, gffn, wr, br)


def _start_row_gather(idx_ref, n_rows, src_hbm, dst, sem):
    def body(r, carry):
        pltpu.make_async_copy(src_hbm.at[pl.ds(idx_ref[0, 0, r], 1)], dst.at[pl.ds(r, 1)], sem).start()
        return carry
    lax.fori_loop(0, n_rows, body, 0, unroll=8)


def _wait_row_gather(n_rows, src_hbm, dst, sem):
    def body(r, carry):
        pltpu.make_async_copy(src_hbm.at[pl.ds(0, 1)], dst.at[pl.ds(r, 1)], sem).wait()
        return carry
    lax.fori_loop(0, n_rows, body, 0, unroll=8)


def _expert_kernel(te_ref, nv_ref, rt_cur, rt_nxt, hf_hbm, w1_ref, w3_ref, w2_ref, y_ref, xbuf, sem):
    s = pl.program_id(0)
    nv = nv_ref[0]
    slot = s % 2
    m = xbuf.shape[1]

    @pl.when(jnp.logical_and(s == 0, nv > 0))
    def _():
        _start_row_gather(rt_cur, m, hf_hbm, xbuf.at[0], sem.at[0])

    @pl.when(s + 1 < nv)
    def _():
        _start_row_gather(rt_nxt, m, hf_hbm, xbuf.at[1 - slot], sem.at[1 - slot])

    @pl.when(s < nv)
    def _():
        _wait_row_gather(m, hf_hbm, xbuf.at[slot], sem.at[slot])
        x = xbuf[slot].astype(BF16)
        h1 = jnp.dot(x, w1_ref[0], preferred_element_type=F32)
        h3 = jnp.dot(x, w3_ref[0], preferred_element_type=F32)
        a = (h1 * _sigmoid(h1) * h3).astype(BF16)
        y_ref[...] = jnp.dot(a, w2_ref[0], preferred_element_type=F32)

    @pl.when(s >= nv)
    def _():
        y_ref[...] = jnp.zeros_like(y_ref)


def _expert_call(tile_expert, n_valid, row_token, hf2d, w1, w3, w2):
    n_tiles = tile_expert.shape[0]
    t, d = hf2d.shape
    de = w1.shape[-1]
    m = TM
    grid_spec = pltpu.PrefetchScalarGridSpec(
        num_scalar_prefetch=2,
        grid=(n_tiles,),
        in_specs=[pl.BlockSpec((1, 1, m), lambda s, te, nv: (s, 0, 0), memory_space=pltpu.SMEM),
                  pl.BlockSpec((1, 1, m), lambda s, te, nv: (jnp.minimum(s + 1, n_tiles - 1), 0, 0),
                               memory_space=pltpu.SMEM),
                  pl.BlockSpec(memory_space=pl.ANY),
                  pl.BlockSpec((1, d, de), lambda s, te, nv: (te[s], 0, 0)),
                  pl.BlockSpec((1, d, de), lambda s, te, nv: (te[s], 0, 0)),
                  pl.BlockSpec((1, de, d), lambda s, te, nv: (te[s], 0, 0))],
        out_specs=pl.BlockSpec((m, d), lambda s, te, nv: (s, 0)),
        scratch_shapes=[pltpu.VMEM((2, m, d), F32), pltpu.SemaphoreType.DMA((2,))],
    )
    return pl.pallas_call(
        _expert_kernel,
        grid_spec=grid_spec,
        out_shape=jax.ShapeDtypeStruct((n_tiles * m, d), F32),
        compiler_params=_cparams(1),
        name="moe_experts",
    )(tile_expert, n_valid, row_token, row_token, hf2d, w1, w3, w2)


def _combine_kernel(pos_cur, pos_nxt, x1_ref, mod_ref, ri_ref, gfin_ref, y_hbm, o_ref, ybuf, sem, *, final):
    bi, i = pl.program_id(0), pl.program_id(1)
    nb, ni = pl.num_programs(0), pl.num_programs(1)
    step = bi * ni + i
    slot = step % 2
    rows = ybuf.shape[1]

    @pl.when(step == 0)
    def _():
        _start_row_gather(pos_cur, rows, y_hbm, ybuf.at[0], sem.at[0])

    @pl.when(step + 1 < nb * ni)
    def _():
        _start_row_gather(pos_nxt, rows, y_hbm, ybuf.at[1 - slot], sem.at[1 - slot])

    _wait_row_gather(rows, y_hbm, ybuf.at[slot], sem.at[slot])
    g_f = mod_ref[0, 0][5:6]
    ri = ri_ref[0]
    p1, p2 = ri[:, 2:3], ri[:, 3:4]
    half = rows // 2
    y = p1 * ybuf[slot, 0:half, :] + p2 * ybuf[slot, half:rows, :]
    x2 = x1_ref[0] + g_f * y
    o_ref[0] = _rms(x2, gfin_ref[...]) if final else x2


def _combine_call(pos, x1, mod6, rinfo, gfin, ysorted, n_ctx_tiles, final):
    b, n, d = x1.shape
    nt = n // TM
    i0 = n_ctx_tiles if final else 0
    ni = nt - i0

    def pos_tile(step):
        return (step // ni) * nt + step % ni + i0

    def tok(width):
        return pl.BlockSpec((1, TM, width), lambda bi, i: (bi, i + i0, 0))

    kern = functools.partial(_combine_kernel, final=final)
    return pl.pallas_call(
        kern,
        grid=(b, ni),
        in_specs=[pl.BlockSpec((1, 1, 2 * TM), lambda bi, i: (pos_tile(bi * ni + i), 0, 0), memory_space=pltpu.SMEM),
                  pl.BlockSpec((1, 1, 2 * TM),
                               lambda bi, i: (pos_tile(jnp.minimum(bi * ni + i + 1, b * ni - 1)), 0, 0),
                               memory_space=pltpu.SMEM),
                  tok(d),
                  pl.BlockSpec((1, 1, 6, d), lambda bi, i: (bi, jnp.minimum((i + i0) // n_ctx_tiles, 1), 0, 0)),
                  tok(LANES),
                  pl.BlockSpec((1, d), lambda bi, i: (0, 0)),
                  pl.BlockSpec(memory_space=pl.ANY)],
        out_specs=pl.BlockSpec((1, TM, d), lambda bi, i: (bi, i, 0)),
        out_shape=jax.ShapeDtypeStruct((b, ni * TM, d), F32),
        scratch_shapes=[pltpu.VMEM((2, 2 * TM, d), F32), pltpu.SemaphoreType.DMA((2,))],
        compiler_params=_cparams(2),
        name="moe_combine_final" if final else "moe_combine",
    )(pos, pos, x1, mod6, rinfo, gfin, ysorted)


def _route(rinfo, b, n):
    t = b * n
    m = TM
    e = rinfo[..., :2].astype(jnp.int32).reshape(t, 2)
    ef = e.T.reshape(-1)
    oh = (ef[:, None] == jnp.arange(N_EXPERTS, dtype=jnp.int32)[None, :]).astype(jnp.int32)
    csum = jnp.cumsum(oh, axis=0)
    rank = jnp.take_along_axis(csum, ef[:, None], axis=1)[:, 0] - 1
    counts = csum[-1]
    padded = ((counts + m - 1) // m) * m
    ends = jnp.cumsum(padded)
    pos = (ends - padded)[ef] + rank
    n_tiles = (2 * t) // m + N_EXPERTS
    n_valid = (ends[-1] // m).astype(jnp.int32).reshape(1)
    tile_expert = jnp.minimum(jnp.searchsorted(ends, jnp.arange(n_tiles, dtype=jnp.int32) * m, side="right"),
                              N_EXPERTS - 1).astype(jnp.int32)
    row_token = jnp.zeros((n_tiles * m,), jnp.int32).at[pos].set(jnp.arange(2 * t, dtype=jnp.int32) % t)
    pos_tiles = pos.reshape(2, t // m, 1, m).transpose(1, 2, 0, 3).reshape(t // m, 1, 2 * m)
    return tile_expert, n_valid, row_token.reshape(n_tiles, 1, m), pos_tiles


def _prep_w_in(w_in):
    d = w_in.shape[0]
    o = np.cumsum([0, LRU_WIDTH, MLA_KV_RANK, MLA_ROPE, GQA_KV_HEADS * GQA_DIM, GQA_KV_HEADS * GQA_DIM,
                   LRU_WIDTH, MLA_Q_RANK, GQA_HEADS * GQA_DIM, N_BRANCH * d])
    xr, ckv, kr, gk, gv, rg, cq, gq, mg = [w_in[:, int(o[i]):int(o[i + 1])] for i in range(9)]
    z = lambda w: jnp.zeros((d, w), w_in.dtype)
    return jnp.concatenate([xr, rg, mg, gq, cq, ckv, z(ROPE_LANE0), kr, z(LANES - ROPE_LANE0 - MLA_ROPE), gk, gv],
                           axis=1).astype(BF16)


def _prep_wuq(wuq):
    r = wuq.shape[0]
    w = wuq.reshape(r, MLA_HEADS, MLA_NOPE + MLA_ROPE)
    w = jnp.pad(w, ((0, 0), (0, 0), (0, MLA_SLOT - MLA_NOPE - MLA_ROPE)))
    return w.reshape(r, MLA_HEADS * MLA_SLOT).astype(BF16)


def _prep_wukv(wukv):
    r = wukv.shape[0]
    w = wukv.reshape(r, MLA_HEADS, MLA_NOPE + MLA_V)
    k = jnp.pad(w[:, :, :MLA_NOPE], ((0, 0), (0, 0), (0, MLA_SLOT - MLA_NOPE))).reshape(r, MLA_HEADS * MLA_SLOT)
    v = w[:, :, MLA_NOPE:].reshape(r, MLA_HEADS * MLA_V)
    return jnp.concatenate([k, v], axis=1).astype(BF16)


def _place_matrix():
    p = np.zeros((LANES, MLA_HEADS * MLA_SLOT), np.float32)
    for hh in range(MLA_HEADS):
        for r in range(MLA_ROPE):
            p[ROPE_LANE0 + r, hh * MLA_SLOT + ROPE_LANE0 + r] = 1.0
    return jnp.asarray(p, BF16)


def _rep_matrix():
    kvw, gw = GQA_KV_HEADS * GQA_DIM, GQA_GROUP * GQA_DIM
    p = np.zeros((GQA_KV_HEADS, kvw, gw), np.float32)
    for g in range(GQA_KV_HEADS):
        for c in range(gw):
            p[g, g * GQA_DIM + c % GQA_DIM, c] = 1.0
    return jnp.asarray(p, BF16)


def _block_diag_pairs(w):
    per = LANES // LRU_BLOCK
    nd = w.shape[0]
    w = w.reshape(nd, LRU_BLOCKS // per, per, LRU_BLOCK, LRU_BLOCK)
    eye = jnp.eye(per, dtype=w.dtype)
    bd = jnp.einsum("dgpij,pq->dgpiqj", w, eye)
    return bd.reshape(nd, LRU_BLOCKS // per, LANES, LANES)


def kernel(x, c, ctx, c_ctx, w_mod, b_mod, g_mix, g_ffn, w_in, conv_w, conv_b, lru_wa, lru_ba, lru_wi, lru_bi,
           lru_lambda, mla_gq, mla_wuq, mla_gkv, mla_wukv, gqa_gq, gqa_gk, w_branch, w_out, moe_wg, moe_bg,
           moe_we, moe_be, moe_w1, moe_w3, moe_w2, g_final):
    b, seq, d = x.shape
    n_ctx = ctx.shape[1]
    n = n_ctx + seq
    depth = w_mod.shape[0]
    assert n_ctx % TM == 0 and seq % KC == 0 and seq % GRID_W == 0 and d == 1024
    n_ctx_tiles = n_ctx // TM

    xx = jnp.concatenate([ctx, x], axis=1)
    tab_m, tab_g = _rope_tables(n_ctx, seq)
    place = _place_matrix()
    rep = _rep_matrix()
    mod_rows = 16
    cc = jnp.concatenate([c, c_ctx[None, :], jnp.zeros((mod_rows - b - 1, d), F32)], axis=0)

    out = None
    for l in range(depth):
        mod = _mod_call(cc, w_mod[l], b_mod[l])
        mod6 = jnp.stack([jnp.broadcast_to(mod[b].reshape(1, 6, d), (b, 6, d)), mod[:b].reshape(b, 6, d)], axis=1)

        xr, rg, mg, qm, km, vm, qg, kg, vg = _inproj_call(
            xx, mod6, g_mix[l].reshape(1, d), _prep_w_in(w_in[l]), _prep_wuq(mla_wuq[l]), _prep_wukv(mla_wukv[l]),
            place, mla_gq[l].reshape(1, -1), mla_gkv[l].reshape(1, -1),
            jnp.tile(gqa_gq[l], LANES // GQA_DIM).reshape(1, LANES),
            jnp.tile(gqa_gk[l], LANES // GQA_DIM).reshape(1, LANES), tab_m, tab_g, n_ctx_tiles)

        xr_t = jnp.transpose(xr, (1, 0, 2))
        wa_bd, wi_bd = _block_diag_pairs(lru_wa[l]), _block_diag_pairs(lru_wi[l])
        lru_args = lambda dr: (conv_w[l], conv_b[l].reshape(1, -1), wa_bd[dr], wi_bd[dr],
                               lru_ba[l, dr].reshape(1, -1), lru_bi[l, dr].reshape(1, -1),
                               lru_lambda[l, dr].reshape(1, -1))
        y_f = _lru_call(xr_t, None, *lru_args(0), n_ctx, False)
        y_t = _lru_call(xr_t, y_f, *lru_args(1), n_ctx, True)
        ylru = jnp.transpose(y_t, (1, 0, 2))

        ym = _mla_attn_call(qm, km, vm, n_ctx)
        yg = _gqa_attn_call(qg, kg, vg, rep, n_ctx)

        wr = jnp.concatenate([moe_wg[l], moe_we[l], jnp.zeros((d, LANES - N_GROUPS - N_EXPERTS), F32)], axis=1)
        br = jnp.concatenate([moe_bg[l], moe_be[l], jnp.zeros((LANES - N_GROUPS - N_EXPERTS,), F32)]).reshape(1, LANES)
        x1, hf, rinfo = _merge_call(xx, mod6, ylru, rg, ym, yg, mg, w_branch[l].astype(BF16), w_out[l].astype(BF16),
                                    g_ffn[l].reshape(1, d), wr, br, n_ctx_tiles)

        tile_expert, n_valid, row_token, pos_tiles = _route(rinfo, b, n)
        ysorted = _expert_call(tile_expert, n_valid, row_token, hf.reshape(b * n, d),
                               moe_w1[l].astype(BF16), moe_w3[l].astype(BF16), moe_w2[l].astype(BF16))
        final = l == depth - 1
        out = _combine_call(pos_tiles, x1, mod6, rinfo, g_final.reshape(1, d), ysorted, n_ctx_tiles, final)
        xx = out
    return out
```

```python
import functools

import numpy as np
import jax
import jax.numpy as jnp
from jax import lax
from jax.experimental import pallas as pl
from jax.experimental.pallas import tpu as pltpu

F32 = jnp.float32
BF16 = jnp.bfloat16

EPS = 1e-6
ROPE_THETA = 10000.0
GRID_W = 64
LOG2E = 1.4426950408889634

LRU_WIDTH = 512
LRU_BLOCKS = 8
LRU_BLOCK = LRU_WIDTH // LRU_BLOCKS
CONV_W = 4
LRU_C = 8.0
MLA_HEADS = 8
MLA_Q_RANK = 256
MLA_KV_RANK = 128
MLA_NOPE = 64
MLA_ROPE = 32
MLA_V = 64
MLA_SCALE = (MLA_NOPE + MLA_ROPE) ** -0.5
GQA_HEADS = 8
GQA_KV_HEADS = 2
GQA_GROUP = GQA_HEADS // GQA_KV_HEADS
GQA_DIM = 64
GQA_SCALE = GQA_DIM ** -0.5
N_BRANCH = 3
BRANCH_W = 512
N_GROUPS = 4
EXPERTS_PER_GROUP = 8
N_EXPERTS = N_GROUPS * EXPERTS_PER_GROUP
D_EXPERT = 256

LANES = 128
TM = 256
KC = 1024
MLA_RQ = 512
MLA_HP = 4
GQA_RQ = 512
VMEM_LIMIT = 56 * 1024 * 1024

C_XR = 0
C_RG = C_XR + LRU_WIDTH
C_MG = C_RG + LRU_WIDTH
C_GQ = C_MG + N_BRANCH * 1024
C_SMALL = C_GQ + GQA_HEADS * GQA_DIM
S_CQ = 0
S_CKV = S_CQ + MLA_Q_RANK
S_KR = S_CKV + MLA_KV_RANK
S_GK = S_KR + LANES
S_GV = S_GK + GQA_KV_HEADS * GQA_DIM
SMALL_W = S_GV + GQA_KV_HEADS * GQA_DIM
MLA_SLOT = 128
ROPE_LANE0 = MLA_NOPE


def _cparams(n_axes):
    return pltpu.CompilerParams(dimension_semantics=("arbitrary",) * n_axes,
                                vmem_limit_bytes=VMEM_LIMIT)


def _rms(x, g):
    return x * lax.rsqrt(jnp.mean(x * x, axis=-1, keepdims=True) + EPS) * g


def _sigmoid(x):
    return 1.0 / (1.0 + jnp.exp(-x))


def _lane_iota(shape):
    return lax.broadcasted_iota(jnp.int32, shape, len(shape) - 1)


def _mod_kernel(c_ref, w_ref, b_ref, o_ref):
    c = c_ref[...]
    s = c * _sigmoid(c)
    o_ref[...] = jnp.dot(s, w_ref[...], preferred_element_type=F32) + b_ref[...]


def _mod_call(cc, w, b):
    rows, d = cc.shape
    n = w.shape[1]
    bn = 512
    return pl.pallas_call(
        _mod_kernel,
        grid=(n // bn,),
        in_specs=[pl.BlockSpec((rows, d), lambda j: (0, 0)),
                  pl.BlockSpec((d, bn), lambda j: (0, j)),
                  pl.BlockSpec((1, bn), lambda j: (0, j))],
        out_specs=pl.BlockSpec((rows, bn), lambda j: (0, j)),
        out_shape=jax.ShapeDtypeStruct((rows, n), F32),
        compiler_params=_cparams(1),
        name="adaln_mod",
    )(cc, w, b.reshape(1, n))


def _rope_tables(n_ctx, seq):
    rows = seq // GRID_W
    row = np.repeat(np.arange(rows), GRID_W).astype(np.float64)
    col = np.tile(np.arange(GRID_W), rows).astype(np.float64)

    def angles(rot_dim):
        quarter = rot_dim // 4
        freqs = ROPE_THETA ** (-np.arange(quarter, dtype=np.float64) / quarter)
        return np.concatenate([row[:, None] * freqs, col[:, None] * freqs], axis=-1)

    n = n_ctx + seq
    am = angles(MLA_ROPE)
    half = MLA_ROPE // 2
    cm = np.ones((n, LANES)); sm = np.zeros((n, LANES))
    cm[:seq, ROPE_LANE0:ROPE_LANE0 + half] = np.cos(am)
    cm[:seq, ROPE_LANE0 + half:ROPE_LANE0 + 2 * half] = np.cos(am)
    sm[:seq, ROPE_LANE0:ROPE_LANE0 + half] = -np.sin(am)
    sm[:seq, ROPE_LANE0 + half:ROPE_LANE0 + 2 * half] = np.sin(am)
    ag = angles(GQA_DIM)
    hg = GQA_DIM // 2
    cg = np.ones((n, LANES)); sg = np.zeros((n, LANES))
    for h0 in (0, GQA_DIM):
        cg[:seq, h0:h0 + hg] = np.cos(ag)
        cg[:seq, h0 + hg:h0 + 2 * hg] = np.cos(ag)
        sg[:seq, h0:h0 + hg] = -np.sin(ag)
        sg[:seq, h0 + hg:h0 + 2 * hg] = np.sin(ag)
    tab_m = np.concatenate([cm, sm], axis=-1).astype(np.float32)
    tab_g = np.concatenate([cg, sg], axis=-1).astype(np.float32)
    return jnp.asarray(tab_m), jnp.asarray(tab_g)


def _rope_mla(y, cos, sin):
    half = MLA_ROPE // 2
    lane = _lane_iota(y.shape)
    rot = jnp.where(lane < ROPE_LANE0 + half,
                    pltpu.roll(y, LANES - half, 1),
                    pltpu.roll(y, half, 1))
    return y * cos + rot * sin


def _rope_gqa(y, cos, sin):
    half = GQA_DIM // 2
    lane = _lane_iota(y.shape)
    rot = jnp.where((lane % GQA_DIM) < half,
                    pltpu.roll(y, LANES - half, 1),
                    pltpu.roll(y, half, 1))
    return y * cos + rot * sin


def _head_rms_pair(t, g):
    lane = _lane_iota(t.shape)
    lo = lane < GQA_DIM
    t2 = t * t
    s_lo = jnp.sum(jnp.where(lo, t2, 0.0), axis=-1, keepdims=True)
    s_hi = jnp.sum(jnp.where(lo, 0.0, t2), axis=-1, keepdims=True)
    r = jnp.where(lo, lax.rsqrt(s_lo * (1.0 / GQA_DIM) + EPS), lax.rsqrt(s_hi * (1.0 / GQA_DIM) + EPS))
    return t * r * g


def _inproj_kernel(x_ref, mod_ref, gmix_ref, w_ref, wuq_ref, wukv_ref, place_ref,
                   gq_ref, gkv_ref, ggq_ref, ggk_ref, tabm_ref, tabg_ref,
                   xr_ref, rg_ref, mg_ref, qm_ref, km_ref, vm_ref, qg_ref, kg_ref, vg_ref):
    x = x_ref[0]
    m6 = mod_ref[0, 0]
    sh, sc = m6[0:1], m6[1:2]
    h = (_rms(x, gmix_ref[...]) * (1.0 + sc) + sh).astype(BF16)

    def proj(c0, width):
        return jnp.dot(h, w_ref[:, c0:c0 + width], preferred_element_type=F32)

    xr_ref[0] = proj(C_XR, LRU_WIDTH)
    rg_ref[0] = proj(C_RG, LRU_WIDTH).astype(BF16)
    for j in range(N_BRANCH * 1024 // 512):
        mg_ref[0, :, j * 512:(j + 1) * 512] = proj(C_MG + j * 512, 512).astype(BF16)

    cos_m, sin_m = tabm_ref[:, :LANES], tabm_ref[:, LANES:]
    cos_g, sin_g = tabg_ref[:, :LANES], tabg_ref[:, LANES:]
    lane = _lane_iota((TM, LANES))

    gq = proj(C_GQ, GQA_HEADS * GQA_DIM)
    for v in range(GQA_HEADS * GQA_DIM // LANES):
        t = gq[:, v * LANES:(v + 1) * LANES]
        y = _rope_gqa(_head_rms_pair(t, ggq_ref[...]), cos_g, sin_g) * (GQA_SCALE * LOG2E)
        ysw = pltpu.roll(y, GQA_DIM, 1)
        qg_ref[0, :, (2 * v) * LANES:(2 * v + 1) * LANES] = jnp.where(lane < GQA_DIM, y, ysw).astype(BF16)
        qg_ref[0, :, (2 * v + 1) * LANES:(2 * v + 2) * LANES] = jnp.where(lane < GQA_DIM, ysw, y).astype(BF16)

    small = proj(C_SMALL, SMALL_W)
    cq = small[:, S_CQ:S_CQ + MLA_Q_RANK]
    ckv = small[:, S_CKV:S_CKV + MLA_KV_RANK]
    krp = small[:, S_KR:S_KR + LANES]
    gk = small[:, S_GK:S_GK + LANES]
    gv = small[:, S_GV:S_GV + LANES]

    kg_ref[0] = _rope_gqa(_head_rms_pair(gk, ggk_ref[...]), cos_g, sin_g).astype(BF16)
    vg_ref[0] = gv.astype(BF16)

    cqn = _rms(cq, gq_ref[...]).astype(BF16)
    qu = jnp.dot(cqn, wuq_ref[...], preferred_element_type=F32)
    for hh in range(MLA_HEADS):
        y = _rope_mla(qu[:, hh * MLA_SLOT:(hh + 1) * MLA_SLOT], cos_m, sin_m)
        qm_ref[0, :, hh * MLA_SLOT:(hh + 1) * MLA_SLOT] = (y * (MLA_SCALE * LOG2E)).astype(BF16)

    ckvn = _rms(ckv, gkv_ref[...]).astype(BF16)
    kvu = jnp.dot(ckvn, wukv_ref[...], preferred_element_type=F32)
    kr = _rope_mla(krp, cos_m, sin_m).astype(BF16)
    kr_placed = jnp.dot(kr, place_ref[...], preferred_element_type=F32)
    n_k = MLA_HEADS * MLA_SLOT
    km_ref[0] = (kvu[:, :n_k] + kr_placed).astype(BF16)
    vm_ref[0] = kvu[:, n_k:].astype(BF16)


def _inproj_call(xx, mod6, gmix, w_all, wuq, wukv, place, gq, gkv, ggq, ggk, tab_m, tab_g, n_lat_tiles):
    b, n, d = xx.shape
    nt = n // TM
    ncol = w_all.shape[1]

    def tok(width):
        return pl.BlockSpec((1, TM, width), lambda bi, i: (bi, i, 0))

    def const(shape):
        return pl.BlockSpec(shape, lambda bi, i: (0,) * len(shape))

    out_w = [(LRU_WIDTH, F32), (LRU_WIDTH, BF16), (N_BRANCH * 1024, BF16),
             (MLA_HEADS * MLA_SLOT, BF16), (MLA_HEADS * MLA_SLOT, BF16), (MLA_HEADS * MLA_V, BF16),
             (GQA_HEADS * LANES, BF16), (GQA_KV_HEADS * GQA_DIM, BF16), (GQA_KV_HEADS * GQA_DIM, BF16)]
    return pl.pallas_call(
        _inproj_kernel,
        grid=(b, nt),
        in_specs=[tok(d),
                  pl.BlockSpec((1, 1, 6, d), lambda bi, i: (bi, jnp.where(i < n_lat_tiles, 1, 0), 0, 0)),
                  const((1, d)), const((d, ncol)), const(wuq.shape), const(wukv.shape), const(place.shape),
                  const((1, MLA_Q_RANK)), const((1, MLA_KV_RANK)), const((1, LANES)), const((1, LANES)),
                  pl.BlockSpec((TM, 2 * LANES), lambda bi, i: (i, 0)),
                  pl.BlockSpec((TM, 2 * LANES), lambda bi, i: (i, 0))],
        out_specs=[tok(w) for w, _ in out_w],
        out_shape=[jax.ShapeDtypeStruct((b, n, w), dt) for w, dt in out_w],
        compiler_params=_cparams(2),
        name="in_proj",
    )(xx, mod6, gmix, w_all, wuq, wukv, place, gq, gkv, ggq, ggk, tab_m, tab_g)


def _flash_t(streams, n_main, tail_start, tail_size):
    nt_dims = (((1,), (1,)), ((), ()))
    ppc = KC // TM

    def scores(start, size):
        return tuple(lax.dot_general(k_fn(start, size), q, nt_dims, preferred_element_type=F32)
                     for q, k_fn, _ in streams)

    def softmax_pv(sts, size, piece0, carries):
        stats = []
        for st, (m, l, _) in zip(sts, carries):
            m_new = jnp.maximum(m, jnp.max(st, axis=0, keepdims=True))
            alpha = jnp.exp2(m - m_new)
            p = jnp.exp2(st - m_new)
            stats.append((m_new, alpha, alpha * l + jnp.sum(p, axis=0, keepdims=True), p.astype(BF16)))
        out = []
        for (_, _, vt_fn), (m_new, alpha, l, pb), (_, _, acc) in zip(streams, stats, carries):
            pv = None
            for t in range(size // TM):
                part = jnp.dot(vt_fn(piece0 + t), pb[t * TM:(t + 1) * TM], preferred_element_type=F32)
                pv = part if pv is None else pv + part
            out.append((m_new, l, alpha * acc + pv))
        return tuple(out)

    carries = tuple((jnp.full((1, q.shape[0]), -jnp.inf, F32), jnp.zeros((1, q.shape[0]), F32),
                     jnp.zeros((LANES, q.shape[0]), F32)) for q, _, _ in streams)
    if n_main:
        carries = lax.fori_loop(
            0, n_main, lambda j, c: softmax_pv(scores(pl.multiple_of(j * KC, KC), KC), KC, j * ppc, c), carries)
    carries = softmax_pv(scores(tail_start, tail_size), tail_size, tail_start // TM, carries)
    return [acc * (1.0 / l) for _, l, acc in carries]


def _mla_attn_kernel(q_ref, k_ref, v_ref, o_ref, vt, *, hp, n_main, tail_start, tail_size):
    nk = k_ref.shape[1]

    @pl.when(pl.program_id(2) == 0)
    def _():
        eye = jnp.where(lax.broadcasted_iota(jnp.int32, (LANES, LANES), 0) == _lane_iota((LANES, LANES)),
                        1.0, 0.0).astype(BF16)
        for pr in range(hp // 2):
            for c in range(nk // TM):
                v = v_ref[0, c * TM:(c + 1) * TM, pr * LANES:(pr + 1) * LANES]
                vt[pr, c] = lax.dot_general(eye, v, (((1,), (1,)), ((), ())),
                                            preferred_element_type=F32).astype(BF16)

    streams = []
    for hh in range(hp):
        q = q_ref[0, :, hh * MLA_SLOT:(hh + 1) * MLA_SLOT]
        k_fn = functools.partial(lambda st, sz, c0: k_ref[0, pl.ds(st, sz), c0:c0 + MLA_SLOT], c0=hh * MLA_SLOT)
        vt_fn = functools.partial(lambda piece, pr: vt[pr, piece], pr=hh // 2)
        streams.append((q, k_fn, vt_fn))
    outs = _flash_t(streams, n_main, tail_start, tail_size)
    row = lax.broadcasted_iota(jnp.int32, outs[0].shape, 0)
    for pr in range(hp // 2):
        o_t = jnp.where(row < MLA_V, outs[2 * pr], outs[2 * pr + 1])
        o_ref[0, :, pr * LANES:(pr + 1) * LANES] = o_t.T.astype(BF16)


def _mla_attn_call(qm, km, vm, seq, n_ctx, ctx_queries):
    b = qm.shape[0]
    hp = MLA_HP
    if ctx_queries:
        rq, nq, q0, nk, k0 = n_ctx, 1, seq // n_ctx, n_ctx, seq // n_ctx
        plan = dict(n_main=0, tail_start=0, tail_size=n_ctx)
    else:
        rq, nq, q0, nk, k0 = MLA_RQ, seq // MLA_RQ, 0, seq + n_ctx, 0
        plan = dict(n_main=seq // KC, tail_start=seq, tail_size=n_ctx)
    kern = functools.partial(_mla_attn_kernel, hp=hp, **plan)
    return pl.pallas_call(
        kern,
        grid=(b, MLA_HEADS // hp, nq),
        in_specs=[pl.BlockSpec((1, rq, hp * MLA_SLOT), lambda bi, p, i: (bi, i + q0, p)),
                  pl.BlockSpec((1, nk, hp * MLA_SLOT), lambda bi, p, i: (bi, k0, p)),
                  pl.BlockSpec((1, nk, hp * MLA_V), lambda bi, p, i: (bi, k0, p))],
        out_specs=pl.BlockSpec((1, rq, hp * MLA_V), lambda bi, p, i: (bi, i, p)),
        out_shape=jax.ShapeDtypeStruct((b, rq * nq, MLA_HEADS * MLA_V), BF16),
        scratch_shapes=[pltpu.VMEM((hp // 2, nk // TM, LANES, TM), BF16)],
        compiler_params=_cparams(3),
        name="mla_attention_ctx" if ctx_queries else "mla_attention",
    )(qm, km, vm)


def _gqa_attn_kernel(q_ref, k_ref, v_ref, selk_ref, selvt_ref, o_ref, kd, vt, *, n_main, tail_start, tail_size):
    i = pl.program_id(2)
    nk = kd.shape[0]
    rq = q_ref.shape[1]

    @pl.when(i == 0)
    def _():
        for c in range(nk // TM):
            rows = pl.ds(c * TM, TM)
            kd[rows, :] = jnp.dot(k_ref[0, rows, :], selk_ref[0], preferred_element_type=F32).astype(BF16)
            vt[c] = lax.dot_general(selvt_ref[0], v_ref[0, rows, :], (((1,), (1,)), ((), ())),
                                    preferred_element_type=F32).astype(BF16)

    streams = [(q_ref[0, :, j * LANES:(j + 1) * LANES], lambda st, sz: kd[pl.ds(st, sz), :], lambda piece: vt[piece])
               for j in range(GQA_GROUP)]
    outs = _flash_t(streams, n_main, tail_start, tail_size)
    row = lax.broadcasted_iota(jnp.int32, (LANES, rq), 0)
    for pr in range(GQA_GROUP // 2):
        o_t = jnp.where(row < GQA_DIM, outs[2 * pr], outs[2 * pr + 1])
        o_ref[0, :, pr * LANES:(pr + 1) * LANES] = o_t.T.astype(BF16)


def _gqa_attn_call(qg, kg, vg, selk, selv, seq, n_ctx, ctx_queries):
    b = qg.shape[0]
    gw = GQA_GROUP * GQA_DIM
    kvw = GQA_KV_HEADS * GQA_DIM
    if ctx_queries:
        rq, nq, q0, nk, k0 = n_ctx, 1, seq // n_ctx, n_ctx, seq // n_ctx
        plan = dict(n_main=0, tail_start=0, tail_size=n_ctx)
    else:
        rq, nq, q0, nk, k0 = GQA_RQ, seq // GQA_RQ, 0, seq + n_ctx, 0
        plan = dict(n_main=seq // KC, tail_start=seq, tail_size=n_ctx)
    kern = functools.partial(_gqa_attn_kernel, **plan)
    return pl.pallas_call(
        kern,
        grid=(b, GQA_KV_HEADS, nq),
        in_specs=[pl.BlockSpec((1, rq, GQA_GROUP * LANES), lambda bi, g, i: (bi, i + q0, g)),
                  pl.BlockSpec((1, nk, kvw), lambda bi, g, i: (bi, k0, 0)),
                  pl.BlockSpec((1, nk, kvw), lambda bi, g, i: (bi, k0, 0)),
                  pl.BlockSpec((1, kvw, LANES), lambda bi, g, i: (g, 0, 0)),
                  pl.BlockSpec((1, kvw, LANES), lambda bi, g, i: (g, 0, 0))],
        out_specs=pl.BlockSpec((1, rq, gw), lambda bi, g, i: (bi, i, g)),
        out_shape=jax.ShapeDtypeStruct((b, rq * nq, GQA_HEADS * GQA_DIM), BF16),
        scratch_shapes=[pltpu.VMEM((nk, LANES), BF16), pltpu.VMEM((nk // TM, LANES, TM), BF16)],
        compiler_params=_cparams(3),
        name="gqa_attention_ctx" if ctx_queries else "gqa_attention",
    )(qg, kg, vg, selk, selv)


def _softplus(z):
    return jnp.maximum(z, 0.0) + jnp.log1p(jnp.exp(-jnp.abs(z)))


def _lru_chunk(j, reverse, n_lat_chunks):
    return jnp.where(j == 0, n_lat_chunks, n_lat_chunks - j if reverse else j - 1)


def _lru_kernel(*refs, reverse, n_chunks, n_lat_chunks):
    if reverse:
        (x_ref, hp_ref, hn_ref, cw_ref, cb_ref, wa_ref, wi_ref, ba_ref, bi_ref, lam_ref, yin_ref,
         y_ref, xs, a_s, b_s, h_s) = refs
    else:
        (x_ref, hp_ref, hn_ref, cw_ref, cb_ref, wa_ref, wi_ref, ba_ref, bi_ref, lam_ref,
         y_ref, xs, a_s, b_s, h_s) = refs
        yin_ref = None
    j = pl.program_id(1)
    c = _lru_chunk(j, reverse, n_lat_chunks)
    ch, bsz, lw = x_ref.shape

    @pl.when(j == 0)
    def _():
        h_s[...] = jnp.zeros_like(h_s)

    seq_first = jnp.logical_or(c == 0, c == n_lat_chunks)
    seq_last = jnp.logical_or(c == n_lat_chunks - 1, c == n_chunks - 1)
    xs[0:2] = jnp.where(seq_first, 0.0, hp_ref[...])
    xs[2:2 + ch] = x_ref[...]
    xs[2 + ch:3 + ch] = jnp.where(seq_last, 0.0, hn_ref[...])
    xc = cb_ref[...]
    for tap in range(CONV_W):
        xc = xc + cw_ref[tap:tap + 1, :] * xs[tap:tap + ch]
    x2 = xc.reshape(ch * bsz, lw)
    r = _sigmoid(jnp.dot(x2, wa_ref[0], preferred_element_type=F32) + ba_ref[...])
    gi = _sigmoid(jnp.dot(x2, wi_ref[0], preferred_element_type=F32) + bi_ref[...])
    log_a = (-LRU_C) * r * _softplus(-lam_ref[...])
    a = jnp.exp(log_a)
    a_s[...] = a.reshape(ch, bsz, lw)
    mult = jnp.sqrt(-jnp.tanh(log_a) * (a * a + 1.0))
    b_s[...] = (mult * gi * x2).reshape(ch, bsz, lw)

    def body(tt, h):
        t = ch - 1 - tt if reverse else tt
        h = a_s[t] * h + b_s[t]
        y_ref[t] = h + yin_ref[t] if reverse else h
        return h

    h_s[...] = lax.fori_loop(0, ch, body, h_s[...], unroll=8)


def _lru_call(xr_t, yin, cw, cb, wa_bd, wi_bd, ba, bi, lam, n_ctx, reverse):
    n, bsz, width = xr_t.shape
    ch = TM
    n_chunks = n // ch
    n_lat_chunks = (n - n_ctx) // ch
    groups = width // LANES
    cmap = functools.partial(_lru_chunk, reverse=reverse, n_lat_chunks=n_lat_chunks)

    main = pl.BlockSpec((ch, bsz, LANES), lambda g, j: (cmap(j), 0, g))
    in_specs = [main,
                pl.BlockSpec((2, bsz, LANES), lambda g, j: (jnp.maximum(cmap(j) * (ch // 2) - 1, 0), 0, g)),
                pl.BlockSpec((1, bsz, LANES), lambda g, j: (jnp.minimum((cmap(j) + 1) * ch, n - 1), 0, g)),
                pl.BlockSpec((CONV_W, LANES), lambda g, j: (0, g)),
                pl.BlockSpec((1, LANES), lambda g, j: (0, g)),
                pl.BlockSpec((1, LANES, LANES), lambda g, j: (g, 0, 0)),
                pl.BlockSpec((1, LANES, LANES), lambda g, j: (g, 0, 0)),
                pl.BlockSpec((1, LANES), lambda g, j: (0, g)),
                pl.BlockSpec((1, LANES), lambda g, j: (0, g)),
                pl.BlockSpec((1, LANES), lambda g, j: (0, g))]
    args = [xr_t, xr_t, xr_t, cw, cb, wa_bd, wi_bd, ba, bi, lam]
    if reverse:
        in_specs.append(main)
        args.append(yin)
    kern = functools.partial(_lru_kernel, reverse=reverse, n_chunks=n_chunks, n_lat_chunks=n_lat_chunks)
    return pl.pallas_call(
        kern,
        grid=(groups, n_chunks),
        in_specs=in_specs,
        out_specs=main,
        out_shape=jax.ShapeDtypeStruct((n, bsz, width), F32),
        scratch_shapes=[pltpu.VMEM((ch + 3, bsz, LANES), F32), pltpu.VMEM((ch, bsz, LANES), F32),
                        pltpu.VMEM((ch, bsz, LANES), F32), pltpu.VMEM((bsz, LANES), F32)],
        compiler_params=_cparams(2),
        name="rglru_rev" if reverse else "rglru_fwd",
    )(*args)


def _gelu_tanh(x):
    return 0.5 * x * (1.0 + jnp.tanh(0.7978845608028654 * (x + 0.044715 * (x * x * x))))


def _merge_kernel(x_ref, mod_ref, ylru_ref, rg_ref, yml_ref, ymc_ref, ygl_ref, ygc_ref, mg_ref, wbr_ref, wout_ref,
                  gffn_ref, wr_ref, br_ref, x1_ref, hf_ref, ri_ref, *, n_lat_tiles):
    d = x_ref.shape[-1]
    is_lat = pl.program_id(1) < n_lat_tiles
    m6 = mod_ref[0, 0]
    g_a, sh_f, sc_f = m6[2:3], m6[3:4], m6[4:5]
    y_rnn = (ylru_ref[0] * _gelu_tanh(rg_ref[0].astype(F32))).astype(BF16)
    y_mla = jnp.where(is_lat, yml_ref[0], ymc_ref[0])
    y_gqa = jnp.where(is_lat, ygl_ref[0], ygc_ref[0])
    acc = jnp.zeros((TM, d), F32)
    for kbr, br in enumerate((y_rnn, y_mla, y_gqa)):
        pr = jnp.dot(br, wbr_ref[kbr], preferred_element_type=F32)
        acc = acc + _sigmoid(mg_ref[0, :, kbr * d:(kbr + 1) * d].astype(F32)) * pr
    out = jnp.dot(acc.astype(BF16), wout_ref[...], preferred_element_type=F32)
    x1 = x_ref[0] + g_a * out
    x1_ref[0] = x1
    hf = _rms(x1, gffn_ref[...]) * (1.0 + sc_f) + sh_f
    hf_ref[0] = hf

    hf_hi = hf.astype(BF16)
    hf_lo = (hf - hf_hi.astype(F32)).astype(BF16)
    lg = (jnp.dot(hf_hi, wr_ref[0], preferred_element_type=F32)
          + jnp.dot(hf_lo, wr_ref[0], preferred_element_type=F32)
          + jnp.dot(hf_hi, wr_ref[1], preferred_element_type=F32)) + br_ref[...]
    lane = _lane_iota(lg.shape)
    big = jnp.int32(1 << 20)
    neg = -jnp.inf
    is_g = lane < N_GROUPS
    gl = jnp.where(is_g, lg, neg)
    gmax = jnp.max(gl, axis=-1, keepdims=True)
    gsel = jnp.min(jnp.where(gl == gmax, lane, big), axis=-1, keepdims=True)
    pg = 1.0 / jnp.sum(jnp.where(is_g, jnp.exp(lg - gmax), 0.0), axis=-1, keepdims=True)
    e0 = N_GROUPS + gsel * EXPERTS_PER_GROUP
    el = jnp.where(jnp.logical_and(lane >= e0, lane < e0 + EXPERTS_PER_GROUP), lg, neg)
    v1 = jnp.max(el, axis=-1, keepdims=True)
    i1 = jnp.min(jnp.where(el == v1, lane, big), axis=-1, keepdims=True)
    el2 = jnp.where(lane == i1, neg, el)
    v2 = jnp.max(el2, axis=-1, keepdims=True)
    i2 = jnp.min(jnp.where(el2 == v2, lane, big), axis=-1, keepdims=True)
    t = jnp.exp(v2 - v1)
    p1 = pg / (1.0 + t)
    p2 = p1 * t
    ri = jnp.where(lane == 0, (i1 - N_GROUPS).astype(F32),
                   jnp.where(lane == 1, (i2 - N_GROUPS).astype(F32),
                             jnp.where(lane == 2, p1, jnp.where(lane == 3, p2, 0.0))))
    ri_ref[0] = ri


def _merge_call(xx, mod6, ylru, rg, ym_lat, ym_ctx, yg_lat, yg_ctx, mg, wbr, wout, gffn, wr, br, n_lat_tiles):
    b, n, d = xx.shape
    nt = n // TM

    def tok(width):
        return pl.BlockSpec((1, TM, width), lambda bi, i: (bi, i, 0))

    def lat(width):
        return pl.BlockSpec((1, TM, width), lambda bi, i: (bi, jnp.minimum(i, n_lat_tiles - 1), 0))

    def ctx(width):
        return pl.BlockSpec((1, TM, width), lambda bi, i: (bi, jnp.maximum(i - n_lat_tiles, 0), 0))

    def const(shape):
        return pl.BlockSpec(shape, lambda bi, i: (0,) * len(shape))

    return pl.pallas_call(
        functools.partial(_merge_kernel, n_lat_tiles=n_lat_tiles),
        grid=(b, nt),
        in_specs=[tok(d),
                  pl.BlockSpec((1, 1, 6, d), lambda bi, i: (bi, jnp.where(i < n_lat_tiles, 1, 0), 0, 0)),
                  tok(BRANCH_W), tok(BRANCH_W), lat(BRANCH_W), ctx(BRANCH_W), lat(BRANCH_W), ctx(BRANCH_W),
                  tok(N_BRANCH * d),
                  const(wbr.shape), const(wout.shape), const((1, d)), const(wr.shape), const((1, LANES))],
        out_specs=[tok(d), tok(d), tok(LANES)],
        out_shape=[jax.ShapeDtypeStruct((b, n, d), F32), jax.ShapeDtypeStruct((b, n, d), F32),
                   jax.ShapeDtypeStruct((b, n, LANES), F32)],
        compiler_params=_cparams(2),
        name="merge_router",
    )(xx, mod6, ylru, rg, ym_lat, ym_ctx, yg_lat, yg_ctx, mg, wbr, wout, gffn, wr, br)


def _start_row_gather(idx_ref, n_rows, src_hbm, dst, sem):
    def body(r, carry):
        pltpu.make_async_copy(src_hbm.at[pl.ds(idx_ref[0, 0, r], 1)], dst.at[pl.ds(r, 1)], sem).start()
        return carry
    lax.fori_loop(0, n_rows, body, 0, unroll=8)


def _wait_row_gather(n_rows, src_hbm, dst, sem):
    def body(r, carry):
        pltpu.make_async_copy(src_hbm.at[pl.ds(0, 1)], dst.at[pl.ds(r, 1)], sem).wait()
        return carry
    lax.fori_loop(0, n_rows, body, 0, unroll=8)


def _expert_kernel(te_ref, nv_ref, rt_cur, rt_nxt, hf_hbm, w1_ref, w3_ref, w2_ref, y_ref, xbuf, sem):
    s = pl.program_id(0)
    nv = nv_ref[0]
    slot = s % 2
    m = xbuf.shape[1]

    @pl.when(jnp.logical_and(s == 0, nv > 0))
    def _():
        _start_row_gather(rt_cur, m, hf_hbm, xbuf.at[0], sem.at[0])

    @pl.when(s + 1 < nv)
    def _():
        _start_row_gather(rt_nxt, m, hf_hbm, xbuf.at[1 - slot], sem.at[1 - slot])

    @pl.when(s < nv)
    def _():
        _wait_row_gather(m, hf_hbm, xbuf.at[slot], sem.at[slot])
        x = xbuf[slot].astype(BF16)
        h1 = jnp.dot(x, w1_ref[0], preferred_element_type=F32)
        h3 = jnp.dot(x, w3_ref[0], preferred_element_type=F32)
        a = (h1 * _sigmoid(h1) * h3).astype(BF16)
        y_ref[...] = jnp.dot(a, w2_ref[0], preferred_element_type=F32)

    @pl.when(s >= nv)
    def _():
        y_ref[...] = jnp.zeros_like(y_ref)


def _expert_call(tile_expert, n_valid, row_token, hf2d, w1, w3, w2):
    n_tiles = tile_expert.shape[0]
    t, d = hf2d.shape
    de = w1.shape[-1]
    m = TM
    grid_spec = pltpu.PrefetchScalarGridSpec(
        num_scalar_prefetch=2,
        grid=(n_tiles,),
        in_specs=[pl.BlockSpec((1, 1, m), lambda s, te, nv: (s, 0, 0), memory_space=pltpu.SMEM),
                  pl.BlockSpec((1, 1, m), lambda s, te, nv: (jnp.minimum(s + 1, n_tiles - 1), 0, 0),
                               memory_space=pltpu.SMEM),
                  pl.BlockSpec(memory_space=pl.ANY),
                  pl.BlockSpec((1, d, de), lambda s, te, nv: (te[s], 0, 0)),
                  pl.BlockSpec((1, d, de), lambda s, te, nv: (te[s], 0, 0)),
                  pl.BlockSpec((1, de, d), lambda s, te, nv: (te[s], 0, 0))],
        out_specs=pl.BlockSpec((m, d), lambda s, te, nv: (s, 0)),
        scratch_shapes=[pltpu.VMEM((2, m, d), F32), pltpu.SemaphoreType.DMA((2,))],
    )
    return pl.pallas_call(
        _expert_kernel,
        grid_spec=grid_spec,
        out_shape=jax.ShapeDtypeStruct((n_tiles * m, d), F32),
        compiler_params=_cparams(1),
        name="moe_experts",
    )(tile_expert, n_valid, row_token, row_token, hf2d, w1, w3, w2)


def _combine_kernel(pos_cur, pos_nxt, x1_ref, mod_ref, ri_ref, gfin_ref, y_hbm, o_ref, ybuf, sem, *, final):
    bi, i = pl.program_id(0), pl.program_id(1)
    nb, ni = pl.num_programs(0), pl.num_programs(1)
    step = bi * ni + i
    slot = step % 2
    rows = ybuf.shape[1]

    @pl.when(step == 0)
    def _():
        _start_row_gather(pos_cur, rows, y_hbm, ybuf.at[0], sem.at[0])

    @pl.when(step + 1 < nb * ni)
    def _():
        _start_row_gather(pos_nxt, rows, y_hbm, ybuf.at[1 - slot], sem.at[1 - slot])

    _wait_row_gather(rows, y_hbm, ybuf.at[slot], sem.at[slot])
    g_f = mod_ref[0, 0][5:6]
    ri = ri_ref[0]
    p1, p2 = ri[:, 2:3], ri[:, 3:4]
    half = rows // 2
    y = p1 * ybuf[slot, 0:half, :] + p2 * ybuf[slot, half:rows, :]
    x2 = x1_ref[0] + g_f * y
    o_ref[0] = _rms(x2, gfin_ref[...]) if final else x2


def _combine_call(pos, x1, mod6, rinfo, gfin, ysorted, n_lat_tiles, final):
    b, n, d = x1.shape
    nt = n // TM
    ni = n_lat_tiles if final else nt

    def pos_tile(step):
        return (step // ni) * nt + step % ni

    def tok(width):
        return pl.BlockSpec((1, TM, width), lambda bi, i: (bi, i, 0))

    kern = functools.partial(_combine_kernel, final=final)
    return pl.pallas_call(
        kern,
        grid=(b, ni),
        in_specs=[pl.BlockSpec((1, 1, 2 * TM), lambda bi, i: (pos_tile(bi * ni + i), 0, 0), memory_space=pltpu.SMEM),
                  pl.BlockSpec((1, 1, 2 * TM),
                               lambda bi, i: (pos_tile(jnp.minimum(bi * ni + i + 1, b * ni - 1)), 0, 0),
                               memory_space=pltpu.SMEM),
                  tok(d),
                  pl.BlockSpec((1, 1, 6, d), lambda bi, i: (bi, jnp.where(i < n_lat_tiles, 1, 0), 0, 0)),
                  tok(LANES),
                  pl.BlockSpec((1, d), lambda bi, i: (0, 0)),
                  pl.BlockSpec(memory_space=pl.ANY)],
        out_specs=tok(d),
        out_shape=jax.ShapeDtypeStruct((b, ni * TM, d), F32),
        scratch_shapes=[pltpu.VMEM((2, 2 * TM, d), F32), pltpu.SemaphoreType.DMA((2,))],
        compiler_params=_cparams(2),
        name="moe_combine_final" if final else "moe_combine",
    )(pos, pos, x1, mod6, rinfo, gfin, ysorted)


def _route(rinfo, b, n):
    t = b * n
    m = TM
    e = rinfo[..., :2].astype(jnp.int32).reshape(t, 2)
    ef = e.T.reshape(-1)
    oh = (ef[:, None] == jnp.arange(N_EXPERTS, dtype=jnp.int32)[None, :]).astype(jnp.int32)
    csum = jnp.cumsum(oh, axis=0)
    rank = jnp.take_along_axis(csum, ef[:, None], axis=1)[:, 0] - 1
    counts = csum[-1]
    padded = ((counts + m - 1) // m) * m
    ends = jnp.cumsum(padded)
    pos = (ends - padded)[ef] + rank
    n_tiles = (2 * t) // m + N_EXPERTS
    n_valid = (ends[-1] // m).astype(jnp.int32).reshape(1)
    tile_start = jnp.arange(n_tiles, dtype=jnp.int32) * m
    tile_expert = jnp.minimum(jnp.sum((tile_start[:, None] >= ends[None, :]).astype(jnp.int32), axis=1),
                              N_EXPERTS - 1).astype(jnp.int32)
    row_token = jnp.zeros((n_tiles * m,), jnp.int32).at[pos].set(jnp.arange(2 * t, dtype=jnp.int32) % t)
    pos_tiles = pos.reshape(2, t // m, 1, m).transpose(1, 2, 0, 3).reshape(t // m, 1, 2 * m)
    return tile_expert, n_valid, row_token.reshape(n_tiles, 1, m), pos_tiles


def _prep_w_in(w_in):
    d = w_in.shape[0]
    o = np.cumsum([0, LRU_WIDTH, MLA_KV_RANK, MLA_ROPE, GQA_KV_HEADS * GQA_DIM, GQA_KV_HEADS * GQA_DIM,
                   LRU_WIDTH, MLA_Q_RANK, GQA_HEADS * GQA_DIM, N_BRANCH * d])
    xr, ckv, kr, gk, gv, rg, cq, gq, mg = [w_in[:, int(o[i]):int(o[i + 1])] for i in range(9)]
    z = lambda w: jnp.zeros((d, w), w_in.dtype)
    return jnp.concatenate([xr, rg, mg, gq, cq, ckv, z(ROPE_LANE0), kr, z(LANES - ROPE_LANE0 - MLA_ROPE), gk, gv],
                           axis=1).astype(BF16)


def _prep_wuq(wuq):
    r = wuq.shape[0]
    w = wuq.reshape(r, MLA_HEADS, MLA_NOPE + MLA_ROPE)
    w = jnp.pad(w, ((0, 0), (0, 0), (0, MLA_SLOT - MLA_NOPE - MLA_ROPE)))
    return w.reshape(r, MLA_HEADS * MLA_SLOT).astype(BF16)


def _prep_wukv(wukv):
    r = wukv.shape[0]
    w = wukv.reshape(r, MLA_HEADS, MLA_NOPE + MLA_V)
    k = jnp.pad(w[:, :, :MLA_NOPE], ((0, 0), (0, 0), (0, MLA_SLOT - MLA_NOPE))).reshape(r, MLA_HEADS * MLA_SLOT)
    v = w[:, :, MLA_NOPE:].reshape(r, MLA_HEADS * MLA_V)
    return jnp.concatenate([k, v], axis=1).astype(BF16)


def _place_matrix():
    p = np.zeros((LANES, MLA_HEADS * MLA_SLOT), np.float32)
    for hh in range(MLA_HEADS):
        for r in range(MLA_ROPE):
            p[ROPE_LANE0 + r, hh * MLA_SLOT + ROPE_LANE0 + r] = 1.0
    return jnp.asarray(p, BF16)


def _kv_select_matrices():
    kvw = GQA_KV_HEADS * GQA_DIM
    sk = np.zeros((GQA_KV_HEADS, kvw, LANES), np.float32)
    sv = np.zeros((GQA_KV_HEADS, kvw, LANES), np.float32)
    for g in range(GQA_KV_HEADS):
        for c in range(GQA_DIM):
            sk[g, g * GQA_DIM + c, c] = 1.0
            sv[g, g * GQA_DIM + c, c] = 1.0
            sv[g, g * GQA_DIM + c, GQA_DIM + c] = 1.0
    return jnp.asarray(sk, BF16), jnp.asarray(sv.transpose(0, 2, 1), BF16)


def _block_diag_pairs(w):
    per = LANES // LRU_BLOCK
    nd = w.shape[0]
    w = w.reshape(nd, LRU_BLOCKS // per, per, LRU_BLOCK, LRU_BLOCK)
    eye = jnp.eye(per, dtype=w.dtype)
    bd = jnp.einsum("dgpij,pq->dgpiqj", w, eye)
    return bd.reshape(nd, LRU_BLOCKS // per, LANES, LANES)


def kernel(x, c, ctx, c_ctx, w_mod, b_mod, g_mix, g_ffn, w_in, conv_w, conv_b, lru_wa, lru_ba, lru_wi, lru_bi,
           lru_lambda, mla_gq, mla_wuq, mla_gkv, mla_wukv, gqa_gq, gqa_gk, w_branch, w_out, moe_wg, moe_bg,
           moe_we, moe_be, moe_w1, moe_w3, moe_w2, g_final):
    b, seq, d = x.shape
    n_ctx = ctx.shape[1]
    n = n_ctx + seq
    depth = w_mod.shape[0]
    assert n_ctx == TM and seq % KC == 0 and seq % GRID_W == 0 and d == 1024
    n_lat_tiles = seq // TM

    xx = jnp.concatenate([x, ctx], axis=1)
    tab_m, tab_g = _rope_tables(n_ctx, seq)
    place = _place_matrix()
    selk, selv = _kv_select_matrices()
    mod_rows = 16
    cc = jnp.concatenate([c, c_ctx[None, :], jnp.zeros((mod_rows - b - 1, d), F32)], axis=0)

    out = None
    for l in range(depth):
        mod = _mod_call(cc, w_mod[l], b_mod[l])
        mod6 = jnp.stack([jnp.broadcast_to(mod[b].reshape(1, 6, d), (b, 6, d)), mod[:b].reshape(b, 6, d)], axis=1)

        xr, rg, mg, qm, km, vm, qg, kg, vg = _inproj_call(
            xx, mod6, g_mix[l].reshape(1, d), _prep_w_in(w_in[l]), _prep_wuq(mla_wuq[l]), _prep_wukv(mla_wukv[l]),
            place, mla_gq[l].reshape(1, -1), mla_gkv[l].reshape(1, -1),
            jnp.tile(gqa_gq[l], LANES // GQA_DIM).reshape(1, LANES),
            jnp.tile(gqa_gk[l], LANES // GQA_DIM).reshape(1, LANES), tab_m, tab_g, n_lat_tiles)

        xr_t = jnp.transpose(xr, (1, 0, 2))
        wa_bd, wi_bd = _block_diag_pairs(lru_wa[l]), _block_diag_pairs(lru_wi[l])
        lru_args = lambda dr: (conv_w[l], conv_b[l].reshape(1, -1), wa_bd[dr], wi_bd[dr],
                               lru_ba[l, dr].reshape(1, -1), lru_bi[l, dr].reshape(1, -1),
                               lru_lambda[l, dr].reshape(1, -1))
        y_f = _lru_call(xr_t, None, *lru_args(0), n_ctx, False)
        y_t = _lru_call(xr_t, y_f, *lru_args(1), n_ctx, True)
        ylru = jnp.transpose(y_t, (1, 0, 2))

        ym_lat = _mla_attn_call(qm, km, vm, seq, n_ctx, False)
        ym_ctx = _mla_attn_call(qm, km, vm, seq, n_ctx, True)
        yg_lat = _gqa_attn_call(qg, kg, vg, selk, selv, seq, n_ctx, False)
        yg_ctx = _gqa_attn_call(qg, kg, vg, selk, selv, seq, n_ctx, True)

        wr = jnp.concatenate([moe_wg[l], moe_we[l], jnp.zeros((d, LANES - N_GROUPS - N_EXPERTS), F32)], axis=1)
        wr_hi = wr.astype(BF16)
        wr = jnp.stack([wr_hi, (wr - wr_hi.astype(F32)).astype(BF16)])
        br = jnp.concatenate([moe_bg[l], moe_be[l], jnp.zeros((LANES - N_GROUPS - N_EXPERTS,), F32)]).reshape(1, LANES)
        x1, hf, rinfo = _merge_call(xx, mod6, ylru, rg, ym_lat, ym_ctx, yg_lat, yg_ctx, mg,
                                    w_branch[l].astype(BF16), w_out[l].astype(BF16),
                                    g_ffn[l].reshape(1, d), wr, br, n_lat_tiles)

        tile_expert, n_valid, row_token, pos_tiles = _route(rinfo, b, n)
        ysorted = _expert_call(tile_expert, n_valid, row_token, hf.reshape(b * n, d),
                               moe_w1[l].astype(BF16), moe_w3[l].astype(BF16), moe_w2[l].astype(BF16))
        final = l == depth - 1
        out = _combine_call(pos_tiles, x1, mod6, rinfo, g_final.reshape(1, d), ysorted, n_lat_tiles, final)
        xx = out
    return out
```

```python
import functools

import numpy as np
import jax
import jax.numpy as jnp
from jax import lax
from jax.experimental import pallas as pl
from jax.experimental.pallas import tpu as pltpu

F32 = jnp.float32
BF16 = jnp.bfloat16

EPS = 1e-6
ROPE_THETA = 10000.0
GRID_W = 64
LOG2E = 1.4426950408889634

LRU_WIDTH = 512
LRU_BLOCKS = 8
LRU_BLOCK = LRU_WIDTH // LRU_BLOCKS
CONV_W = 4
LRU_C = 8.0
MLA_HEADS = 8
MLA_Q_RANK = 256
MLA_KV_RANK = 128
MLA_NOPE = 64
MLA_ROPE = 32
MLA_V = 64
MLA_SCALE = (MLA_NOPE + MLA_ROPE) ** -0.5
GQA_HEADS = 8
GQA_KV_HEADS = 2
GQA_GROUP = GQA_HEADS // GQA_KV_HEADS
GQA_DIM = 64
GQA_SCALE = GQA_DIM ** -0.5
N_BRANCH = 3
BRANCH_W = 512
N_GROUPS = 4
EXPERTS_PER_GROUP = 8
N_EXPERTS = N_GROUPS * EXPERTS_PER_GROUP
D_EXPERT = 256

LANES = 128
TM = 256
KC = 1024
MLA_RQ = 512
MLA_HP = 4
GQA_RQ = 512
VMEM_LIMIT = 56 * 1024 * 1024

C_XR = 0
C_RG = C_XR + LRU_WIDTH
C_MG = C_RG + LRU_WIDTH
C_GQ = C_MG + N_BRANCH * 1024
C_SMALL = C_GQ + GQA_HEADS * GQA_DIM
S_CQ = 0
S_CKV = S_CQ + MLA_Q_RANK
S_KR = S_CKV + MLA_KV_RANK
S_GK = S_KR + LANES
S_GV = S_GK + GQA_KV_HEADS * GQA_DIM
SMALL_W = S_GV + GQA_KV_HEADS * GQA_DIM
MLA_SLOT = 128
ROPE_LANE0 = MLA_NOPE
ONES_ROW = 64
assert MLA_V == ONES_ROW and GQA_DIM == ONES_ROW


def _cparams(n_axes):
    return pltpu.CompilerParams(dimension_semantics=("arbitrary",) * n_axes,
                                vmem_limit_bytes=VMEM_LIMIT)


def _rms(x, g):
    return x * lax.rsqrt(jnp.mean(x * x, axis=-1, keepdims=True) + EPS) * g


def _sigmoid(x):
    return 1.0 / (1.0 + jnp.exp(-x))


def _lane_iota(shape):
    return lax.broadcasted_iota(jnp.int32, shape, len(shape) - 1)


def _mod_kernel(c_ref, w_ref, b_ref, o_ref):
    c = c_ref[...]
    s = c * _sigmoid(c)
    o_ref[...] = jnp.dot(s, w_ref[...], preferred_element_type=F32) + b_ref[...]


def _mod_call(cc, w, b):
    rows, d = cc.shape
    n = w.shape[1]
    bn = 512
    return pl.pallas_call(
        _mod_kernel,
        grid=(n // bn,),
        in_specs=[pl.BlockSpec((rows, d), lambda j: (0, 0)),
                  pl.BlockSpec((d, bn), lambda j: (0, j)),
                  pl.BlockSpec((1, bn), lambda j: (0, j))],
        out_specs=pl.BlockSpec((rows, bn), lambda j: (0, j)),
        out_shape=jax.ShapeDtypeStruct((rows, n), F32),
        compiler_params=_cparams(1),
        name="adaln_mod",
    )(cc, w, b.reshape(1, n))


def _rope_tables(n_ctx, seq):
    rows = seq // GRID_W
    row = np.repeat(np.arange(rows), GRID_W).astype(np.float64)
    col = np.tile(np.arange(GRID_W), rows).astype(np.float64)

    def angles(rot_dim):
        quarter = rot_dim // 4
        freqs = ROPE_THETA ** (-np.arange(quarter, dtype=np.float64) / quarter)
        return np.concatenate([row[:, None] * freqs, col[:, None] * freqs], axis=-1)

    n = n_ctx + seq
    am = angles(MLA_ROPE)
    half = MLA_ROPE // 2
    cm = np.ones((n, LANES)); sm = np.zeros((n, LANES))
    cm[:seq, ROPE_LANE0:ROPE_LANE0 + half] = np.cos(am)
    cm[:seq, ROPE_LANE0 + half:ROPE_LANE0 + 2 * half] = np.cos(am)
    sm[:seq, ROPE_LANE0:ROPE_LANE0 + half] = -np.sin(am)
    sm[:seq, ROPE_LANE0 + half:ROPE_LANE0 + 2 * half] = np.sin(am)
    ag = angles(GQA_DIM)
    hg = GQA_DIM // 2
    cg = np.ones((n, LANES)); sg = np.zeros((n, LANES))
    for h0 in (0, GQA_DIM):
        cg[:seq, h0:h0 + hg] = np.cos(ag)
        cg[:seq, h0 + hg:h0 + 2 * hg] = np.cos(ag)
        sg[:seq, h0:h0 + hg] = -np.sin(ag)
        sg[:seq, h0 + hg:h0 + 2 * hg] = np.sin(ag)
    tab_m = np.concatenate([cm, sm], axis=-1).astype(np.float32)
    tab_g = np.concatenate([cg, sg], axis=-1).astype(np.float32)
    return jnp.asarray(tab_m), jnp.asarray(tab_g)


def _rope_mla(y, cos, sin):
    half = MLA_ROPE // 2
    lane = _lane_iota(y.shape)
    rot = jnp.where(lane < ROPE_LANE0 + half,
                    pltpu.roll(y, LANES - half, 1),
                    pltpu.roll(y, half, 1))
    return y * cos + rot * sin


def _rope_gqa(y, cos, sin):
    half = GQA_DIM // 2
    lane = _lane_iota(y.shape)
    rot = jnp.where((lane % GQA_DIM) < half,
                    pltpu.roll(y, LANES - half, 1),
                    pltpu.roll(y, half, 1))
    return y * cos + rot * sin


def _head_rms_pair(t, g):
    lane = _lane_iota(t.shape)
    lo = lane < GQA_DIM
    t2 = t * t
    s_lo = jnp.sum(jnp.where(lo, t2, 0.0), axis=-1, keepdims=True)
    s_hi = jnp.sum(jnp.where(lo, 0.0, t2), axis=-1, keepdims=True)
    r = jnp.where(lo, lax.rsqrt(s_lo * (1.0 / GQA_DIM) + EPS), lax.rsqrt(s_hi * (1.0 / GQA_DIM) + EPS))
    return t * r * g


def _inproj_kernel(x_ref, mod_ref, gmix_ref, w_ref, wuq_ref, wukv_ref, place_ref,
                   gq_ref, gkv_ref, ggq_ref, ggk_ref, tabm_ref, tabg_ref,
                   xr_ref, rg_ref, mg_ref, qm_ref, km_ref, vm_ref, qg_ref, kg_ref, vg_ref):
    x = x_ref[0]
    m6 = mod_ref[0, 0]
    sh, sc = m6[0:1], m6[1:2]
    h = (_rms(x, gmix_ref[...]) * (1.0 + sc) + sh).astype(BF16)

    def proj(c0, width):
        return jnp.dot(h, w_ref[:, c0:c0 + width], preferred_element_type=F32)

    xr_ref[0] = proj(C_XR, LRU_WIDTH)
    rg_ref[0] = proj(C_RG, LRU_WIDTH).astype(BF16)
    for j in range(N_BRANCH * 1024 // 512):
        mg_ref[0, :, j * 512:(j + 1) * 512] = proj(C_MG + j * 512, 512).astype(BF16)

    cos_m, sin_m = tabm_ref[:, :LANES], tabm_ref[:, LANES:]
    cos_g, sin_g = tabg_ref[:, :LANES], tabg_ref[:, LANES:]
    lane = _lane_iota((TM, LANES))

    gq = proj(C_GQ, GQA_HEADS * GQA_DIM)
    for v in range(GQA_HEADS * GQA_DIM // LANES):
        t = gq[:, v * LANES:(v + 1) * LANES]
        y = _rope_gqa(_head_rms_pair(t, ggq_ref[...]), cos_g, sin_g) * (GQA_SCALE * LOG2E)
        ysw = pltpu.roll(y, GQA_DIM, 1)
        qg_ref[0, :, (2 * v) * LANES:(2 * v + 1) * LANES] = jnp.where(lane < GQA_DIM, y, ysw).astype(BF16)
        qg_ref[0, :, (2 * v + 1) * LANES:(2 * v + 2) * LANES] = jnp.where(lane < GQA_DIM, ysw, y).astype(BF16)

    small = proj(C_SMALL, SMALL_W)
    cq = small[:, S_CQ:S_CQ + MLA_Q_RANK]
    ckv = small[:, S_CKV:S_CKV + MLA_KV_RANK]
    krp = small[:, S_KR:S_KR + LANES]
    gk = small[:, S_GK:S_GK + LANES]
    gv = small[:, S_GV:S_GV + LANES]

    kg_ref[0] = _rope_gqa(_head_rms_pair(gk, ggk_ref[...]), cos_g, sin_g).astype(BF16)
    vg_ref[0] = gv.astype(BF16)

    cqn = _rms(cq, gq_ref[...]).astype(BF16)
    qu = jnp.dot(cqn, wuq_ref[...], preferred_element_type=F32)
    for hh in range(MLA_HEADS):
        y = _rope_mla(qu[:, hh * MLA_SLOT:(hh + 1) * MLA_SLOT], cos_m, sin_m)
        qm_ref[0, :, hh * MLA_SLOT:(hh + 1) * MLA_SLOT] = (y * (MLA_SCALE * LOG2E)).astype(BF16)

    ckvn = _rms(ckv, gkv_ref[...]).astype(BF16)
    kvu = jnp.dot(ckvn, wukv_ref[...], preferred_element_type=F32)
    kr = _rope_mla(krp, cos_m, sin_m).astype(BF16)
    kr_placed = jnp.dot(kr, place_ref[...], preferred_element_type=F32)
    n_k = MLA_HEADS * MLA_SLOT
    km_ref[0] = (kvu[:, :n_k] + kr_placed).astype(BF16)
    vm_ref[0] = kvu[:, n_k:].astype(BF16)


def _inproj_call(xx, mod6, gmix, w_all, wuq, wukv, place, gq, gkv, ggq, ggk, tab_m, tab_g, n_lat_tiles):
    b, n, d = xx.shape
    nt = n // TM
    ncol = w_all.shape[1]

    def tok(width):
        return pl.BlockSpec((1, TM, width), lambda bi, i: (bi, i, 0))

    def const(shape):
        return pl.BlockSpec(shape, lambda bi, i: (0,) * len(shape))

    out_w = [(LRU_WIDTH, F32), (LRU_WIDTH, BF16), (N_BRANCH * 1024, BF16),
             (MLA_HEADS * MLA_SLOT, BF16), (MLA_HEADS * MLA_SLOT, BF16), (MLA_HEADS * MLA_V, BF16),
             (GQA_HEADS * LANES, BF16), (GQA_KV_HEADS * GQA_DIM, BF16), (GQA_KV_HEADS * GQA_DIM, BF16)]
    return pl.pallas_call(
        _inproj_kernel,
        grid=(b, nt),
        in_specs=[tok(d),
                  pl.BlockSpec((1, 1, 6, d), lambda bi, i: (bi, jnp.where(i < n_lat_tiles, 1, 0), 0, 0)),
                  const((1, d)), const((d, ncol)), const(wuq.shape), const(wukv.shape), const(place.shape),
                  const((1, MLA_Q_RANK)), const((1, MLA_KV_RANK)), const((1, LANES)), const((1, LANES)),
                  pl.BlockSpec((TM, 2 * LANES), lambda bi, i: (i, 0)),
                  pl.BlockSpec((TM, 2 * LANES), lambda bi, i: (i, 0))],
        out_specs=[tok(w) for w, _ in out_w],
        out_shape=[jax.ShapeDtypeStruct((b, n, w), dt) for w, dt in out_w],
        compiler_params=_cparams(2),
        name="in_proj",
    )(xx, mod6, gmix, w_all, wuq, wukv, place, gq, gkv, ggq, ggk, tab_m, tab_g)


def _flash_t(streams, n_main, tail_start, tail_size):
    nt_dims = (((1,), (1,)), ((), ()))
    ppc = KC // TM

    def scores(start, size):
        return tuple(lax.dot_general(k_fn(start, size), q, nt_dims, preferred_element_type=F32)
                     for q, k_fn, _ in streams)

    def softmax_pv(sts, size, piece0, carries):
        stats = []
        for st, (m, _) in zip(sts, carries):
            m_new = jnp.maximum(m, jnp.max(st, axis=0, keepdims=True))
            stats.append((m_new, jnp.exp2(m - m_new), jnp.exp2(st - m_new).astype(BF16)))
        out = []
        for (_, _, vt_fn), (m_new, alpha, pb), (_, acc) in zip(streams, stats, carries):
            pv = None
            for t in range(size // TM):
                part = jnp.dot(vt_fn(piece0 + t), pb[t * TM:(t + 1) * TM], preferred_element_type=F32)
                pv = part if pv is None else pv + part
            out.append((m_new, alpha * acc + pv))
        return tuple(out)

    carries = tuple((jnp.full((1, q.shape[0]), -jnp.inf, F32), jnp.zeros((LANES, q.shape[0]), F32))
                    for q, _, _ in streams)
    if n_main:
        carries = lax.fori_loop(
            0, n_main, lambda j, c: softmax_pv(scores(pl.multiple_of(j * KC, KC), KC), KC, j * ppc, c), carries)
    carries = softmax_pv(scores(tail_start, tail_size), tail_size, tail_start // TM, carries)
    return [acc[:ONES_ROW] * (1.0 / acc[ONES_ROW:ONES_ROW + 1]) for _, acc in carries]


def _values_t(sel, v, row):
    vt = lax.dot_general(sel, v, (((1,), (1,)), ((), ())), preferred_element_type=F32)
    return jnp.where(row == ONES_ROW, 1.0, vt).astype(BF16)


def _mla_attn_kernel(q_ref, k_ref, v_ref, o_ref, vt, *, hp, n_main, tail_start, tail_size):
    nk = k_ref.shape[1]

    @pl.when(pl.program_id(2) == 0)
    def _():
        row = lax.broadcasted_iota(jnp.int32, (LANES, TM), 0)
        r_i = lax.broadcasted_iota(jnp.int32, (LANES, LANES), 0)
        c_i = _lane_iota((LANES, LANES))
        for hh in range(hp):
            sel = jnp.where(jnp.logical_and(r_i < MLA_V, c_i == r_i + (hh % 2) * MLA_V), 1.0, 0.0).astype(BF16)
            for c in range(nk // TM):
                v = v_ref[0, c * TM:(c + 1) * TM, (hh // 2) * LANES:(hh // 2 + 1) * LANES]
                vt[hh, c] = _values_t(sel, v, row)

    streams = []
    for hh in range(hp):
        q = q_ref[0, :, hh * MLA_SLOT:(hh + 1) * MLA_SLOT]
        k_fn = functools.partial(lambda st, sz, c0: k_ref[0, pl.ds(st, sz), c0:c0 + MLA_SLOT], c0=hh * MLA_SLOT)
        vt_fn = functools.partial(lambda piece, h: vt[h, piece], h=hh)
        streams.append((q, k_fn, vt_fn))
    outs = _flash_t(streams, n_main, tail_start, tail_size)
    for pr in range(hp // 2):
        o_t = jnp.concatenate([outs[2 * pr], outs[2 * pr + 1]], axis=0)
        o_ref[0, :, pr * LANES:(pr + 1) * LANES] = o_t.T.astype(BF16)


def _mla_attn_call(qm, km, vm, seq, n_ctx, ctx_queries):
    b = qm.shape[0]
    hp = MLA_HP
    if ctx_queries:
        rq, nq, q0, nk, k0 = n_ctx, 1, seq // n_ctx, n_ctx, seq // n_ctx
        plan = dict(n_main=0, tail_start=0, tail_size=n_ctx)
    else:
        rq, nq, q0, nk, k0 = MLA_RQ, seq // MLA_RQ, 0, seq + n_ctx, 0
        plan = dict(n_main=seq // KC - 1, tail_start=seq - KC, tail_size=KC + n_ctx)
    kern = functools.partial(_mla_attn_kernel, hp=hp, **plan)
    return pl.pallas_call(
        kern,
        grid=(b, MLA_HEADS // hp, nq),
        in_specs=[pl.BlockSpec((1, rq, hp * MLA_SLOT), lambda bi, p, i: (bi, i + q0, p)),
                  pl.BlockSpec((1, nk, hp * MLA_SLOT), lambda bi, p, i: (bi, k0, p)),
                  pl.BlockSpec((1, nk, hp * MLA_V), lambda bi, p, i: (bi, k0, p))],
        out_specs=pl.BlockSpec((1, rq, hp * MLA_V), lambda bi, p, i: (bi, i, p)),
        out_shape=jax.ShapeDtypeStruct((b, rq * nq, MLA_HEADS * MLA_V), BF16),
        scratch_shapes=[pltpu.VMEM((hp, nk // TM, LANES, TM), BF16)],
        compiler_params=_cparams(3),
        name="mla_attention_ctx" if ctx_queries else "mla_attention",
    )(qm, km, vm)


def _gqa_attn_kernel(q_ref, k_ref, v_ref, selk_ref, selvt_ref, o_ref, kd, vt, *, n_main, tail_start, tail_size):
    i = pl.program_id(2)
    nk = kd.shape[0]
    rq = q_ref.shape[1]

    @pl.when(i == 0)
    def _():
        row = lax.broadcasted_iota(jnp.int32, (LANES, TM), 0)
        for c in range(nk // TM):
            rows = pl.ds(c * TM, TM)
            kd[rows, :] = jnp.dot(k_ref[0, rows, :], selk_ref[0], preferred_element_type=F32).astype(BF16)
            vt[c] = _values_t(selvt_ref[0], v_ref[0, rows, :], row)

    streams = [(q_ref[0, :, j * LANES:(j + 1) * LANES], lambda st, sz: kd[pl.ds(st, sz), :], lambda piece: vt[piece])
               for j in range(GQA_GROUP)]
    outs = _flash_t(streams, n_main, tail_start, tail_size)
    for pr in range(GQA_GROUP // 2):
        o_t = jnp.concatenate([outs[2 * pr], outs[2 * pr + 1]], axis=0)
        o_ref[0, :, pr * LANES:(pr + 1) * LANES] = o_t.T.astype(BF16)


def _gqa_attn_call(qg, kg, vg, selk, selv, seq, n_ctx, ctx_queries):
    b = qg.shape[0]
    gw = GQA_GROUP * GQA_DIM
    kvw = GQA_KV_HEADS * GQA_DIM
    if ctx_queries:
        rq, nq, q0, nk, k0 = n_ctx, 1, seq // n_ctx, n_ctx, seq // n_ctx
        plan = dict(n_main=0, tail_start=0, tail_size=n_ctx)
    else:
        rq, nq, q0, nk, k0 = GQA_RQ, seq // GQA_RQ, 0, seq + n_ctx, 0
        plan = dict(n_main=seq // KC - 1, tail_start=seq - KC, tail_size=KC + n_ctx)
    kern = functools.partial(_gqa_attn_kernel, **plan)
    return pl.pallas_call(
        kern,
        grid=(b, GQA_KV_HEADS, nq),
        in_specs=[pl.BlockSpec((1, rq, GQA_GROUP * LANES), lambda bi, g, i: (bi, i + q0, g)),
                  pl.BlockSpec((1, nk, kvw), lambda bi, g, i: (bi, k0, 0)),
                  pl.BlockSpec((1, nk, kvw), lambda bi, g, i: (bi, k0, 0)),
                  pl.BlockSpec((1, kvw, LANES), lambda bi, g, i: (g, 0, 0)),
                  pl.BlockSpec((1, kvw, LANES), lambda bi, g, i: (g, 0, 0))],
        out_specs=pl.BlockSpec((1, rq, gw), lambda bi, g, i: (bi, i, g)),
        out_shape=jax.ShapeDtypeStruct((b, rq * nq, GQA_HEADS * GQA_DIM), BF16),
        scratch_shapes=[pltpu.VMEM((nk, LANES), BF16), pltpu.VMEM((nk // TM, LANES, TM), BF16)],
        compiler_params=_cparams(3),
        name="gqa_attention_ctx" if ctx_queries else "gqa_attention",
    )(qg, kg, vg, selk, selv)


def _softplus(z):
    return jnp.maximum(z, 0.0) + jnp.log1p(jnp.exp(-jnp.abs(z)))


def _lru_chunk(j, reverse, n_lat_chunks):
    return jnp.where(j == 0, n_lat_chunks, n_lat_chunks - j if reverse else j - 1)


def _lru_kernel(*refs, reverse, n_chunks, n_lat_chunks):
    if reverse:
        (x_ref, hp_ref, hn_ref, cw_ref, cb_ref, wa_ref, wi_ref, ba_ref, bi_ref, lam_ref, yin_ref,
         y_ref, xs, a_s, b_s, h_s) = refs
    else:
        (x_ref, hp_ref, hn_ref, cw_ref, cb_ref, wa_ref, wi_ref, ba_ref, bi_ref, lam_ref,
         y_ref, xs, a_s, b_s, h_s) = refs
        yin_ref = None
    j = pl.program_id(1)
    c = _lru_chunk(j, reverse, n_lat_chunks)
    ch, bsz, lw = x_ref.shape

    @pl.when(j == 0)
    def _():
        h_s[...] = jnp.zeros_like(h_s)

    seq_first = jnp.logical_or(c == 0, c == n_lat_chunks)
    seq_last = jnp.logical_or(c == n_lat_chunks - 1, c == n_chunks - 1)
    xs[0:2] = jnp.where(seq_first, 0.0, hp_ref[...])
    xs[2:2 + ch] = x_ref[...]
    xs[2 + ch:3 + ch] = jnp.where(seq_last, 0.0, hn_ref[...])
    xc = cb_ref[...]
    for tap in range(CONV_W):
        xc = xc + cw_ref[tap:tap + 1, :] * xs[tap:tap + ch]
    x2 = xc.reshape(ch * bsz, lw)
    r = _sigmoid(jnp.dot(x2, wa_ref[0], preferred_element_type=F32) + ba_ref[...])
    gi = _sigmoid(jnp.dot(x2, wi_ref[0], preferred_element_type=F32) + bi_ref[...])
    log_a = (-LRU_C) * r * _softplus(-lam_ref[...])
    a = jnp.exp(log_a)
    a_s[...] = a.reshape(ch, bsz, lw)
    mult = jnp.sqrt(-jnp.tanh(log_a) * (a * a + 1.0))
    b_s[...] = (mult * gi * x2).reshape(ch, bsz, lw)

    def body(tt, h):
        t = ch - 1 - tt if reverse else tt
        h = a_s[t] * h + b_s[t]
        y_ref[t] = h + yin_ref[t] if reverse else h
        return h

    h_s[...] = lax.fori_loop(0, ch, body, h_s[...], unroll=8)


def _lru_call(xr_t, yin, cw, cb, wa_bd, wi_bd, ba, bi, lam, n_ctx, reverse):
    n, bsz, width = xr_t.shape
    ch = TM
    n_chunks = n // ch
    n_lat_chunks = (n - n_ctx) // ch
    groups = width // LANES
    cmap = functools.partial(_lru_chunk, reverse=reverse, n_lat_chunks=n_lat_chunks)

    main = pl.BlockSpec((ch, bsz, LANES), lambda g, j: (cmap(j), 0, g))
    in_specs = [main,
                pl.BlockSpec((2, bsz, LANES), lambda g, j: (jnp.maximum(cmap(j) * (ch // 2) - 1, 0), 0, g)),
                pl.BlockSpec((1, bsz, LANES), lambda g, j: (jnp.minimum((cmap(j) + 1) * ch, n - 1), 0, g)),
                pl.BlockSpec((CONV_W, LANES), lambda g, j: (0, g)),
                pl.BlockSpec((1, LANES), lambda g, j: (0, g)),
                pl.BlockSpec((1, LANES, LANES), lambda g, j: (g, 0, 0)),
                pl.BlockSpec((1, LANES, LANES), lambda g, j: (g, 0, 0)),
                pl.BlockSpec((1, LANES), lambda g, j: (0, g)),
                pl.BlockSpec((1, LANES), lambda g, j: (0, g)),
                pl.BlockSpec((1, LANES), lambda g, j: (0, g))]
    args = [xr_t, xr_t, xr_t, cw, cb, wa_bd, wi_bd, ba, bi, lam]
    if reverse:
        in_specs.append(main)
        args.append(yin)
    kern = functools.partial(_lru_kernel, reverse=reverse, n_chunks=n_chunks, n_lat_chunks=n_lat_chunks)
    return pl.pallas_call(
        kern,
        grid=(groups, n_chunks),
        in_specs=in_specs,
        out_specs=main,
        out_shape=jax.ShapeDtypeStruct((n, bsz, width), F32),
        scratch_shapes=[pltpu.VMEM((ch + 3, bsz, LANES), F32), pltpu.VMEM((ch, bsz, LANES), F32),
                        pltpu.VMEM((ch, bsz, LANES), F32), pltpu.VMEM((bsz, LANES), F32)],
        compiler_params=_cparams(2),
        name="rglru_rev" if reverse else "rglru_fwd",
    )(*args)


def _gelu_tanh(x):
    return 0.5 * x * (1.0 + jnp.tanh(0.7978845608028654 * (x + 0.044715 * (x * x * x))))


def _merge_kernel(x_ref, mod_ref, ylru_ref, rg_ref, yml_ref, ymc_ref, ygl_ref, ygc_ref, mg_ref, wbr_ref, wout_ref,
                  gffn_ref, wr_ref, br_ref, x1_ref, hf_ref, ri_ref, *, n_lat_tiles):
    d = x_ref.shape[-1]
    is_lat = pl.program_id(1) < n_lat_tiles
    m6 = mod_ref[0, 0]
    g_a, sh_f, sc_f = m6[2:3], m6[3:4], m6[4:5]
    y_rnn = (ylru_ref[0] * _gelu_tanh(rg_ref[0].astype(F32))).astype(BF16)
    y_mla = jnp.where(is_lat, yml_ref[0], ymc_ref[0])
    y_gqa = jnp.where(is_lat, ygl_ref[0], ygc_ref[0])
    acc = jnp.zeros((TM, d), F32)
    for kbr, br in enumerate((y_rnn, y_mla, y_gqa)):
        pr = jnp.dot(br, wbr_ref[kbr], preferred_element_type=F32)
        acc = acc + _sigmoid(mg_ref[0, :, kbr * d:(kbr + 1) * d].astype(F32)) * pr
    out = jnp.dot(acc.astype(BF16), wout_ref[...], preferred_element_type=F32)
    x1 = x_ref[0] + g_a * out
    x1_ref[0] = x1
    hf = _rms(x1, gffn_ref[...]) * (1.0 + sc_f) + sh_f
    _to_token_tiles(hf_ref, hf)

    hf_hi = hf.astype(BF16)
    hf_lo = (hf - hf_hi.astype(F32)).astype(BF16)
    lg = (jnp.dot(hf_hi, wr_ref[0], preferred_element_type=F32)
          + jnp.dot(hf_lo, wr_ref[0], preferred_element_type=F32)
          + jnp.dot(hf_hi, wr_ref[1], preferred_element_type=F32)) + br_ref[...]
    lane = _lane_iota(lg.shape)
    big = jnp.int32(1 << 20)
    neg = -jnp.inf
    is_g = lane < N_GROUPS
    gl = jnp.where(is_g, lg, neg)
    gmax = jnp.max(gl, axis=-1, keepdims=True)
    gsel = jnp.min(jnp.where(gl == gmax, lane, big), axis=-1, keepdims=True)
    pg = 1.0 / jnp.sum(jnp.where(is_g, jnp.exp(lg - gmax), 0.0), axis=-1, keepdims=True)
    e0 = N_GROUPS + gsel * EXPERTS_PER_GROUP
    el = jnp.where(jnp.logical_and(lane >= e0, lane < e0 + EXPERTS_PER_GROUP), lg, neg)
    v1 = jnp.max(el, axis=-1, keepdims=True)
    i1 = jnp.min(jnp.where(el == v1, lane, big), axis=-1, keepdims=True)
    el2 = jnp.where(lane == i1, neg, el)
    v2 = jnp.max(el2, axis=-1, keepdims=True)
    i2 = jnp.min(jnp.where(el2 == v2, lane, big), axis=-1, keepdims=True)
    t = jnp.exp(v2 - v1)
    p1 = pg / (1.0 + t)
    p2 = p1 * t
    ri = jnp.where(lane == 0, (i1 - N_GROUPS).astype(F32),
                   jnp.where(lane == 1, (i2 - N_GROUPS).astype(F32),
                             jnp.where(lane == 2, p1, jnp.where(lane == 3, p2, 0.0))))
    ri_ref[0] = ri


def _merge_call(xx, mod6, ylru, rg, ym_lat, ym_ctx, yg_lat, yg_ctx, mg, wbr, wout, gffn, wr, br, n_lat_tiles):
    b, n, d = xx.shape
    nt = n // TM

    def tok(width):
        return pl.BlockSpec((1, TM, width), lambda bi, i: (bi, i, 0))

    def lat(width):
        return pl.BlockSpec((1, TM, width), lambda bi, i: (bi, jnp.minimum(i, n_lat_tiles - 1), 0))

    def ctx(width):
        return pl.BlockSpec((1, TM, width), lambda bi, i: (bi, jnp.maximum(i - n_lat_tiles, 0), 0))

    def const(shape):
        return pl.BlockSpec(shape, lambda bi, i: (0,) * len(shape))

    return pl.pallas_call(
        functools.partial(_merge_kernel, n_lat_tiles=n_lat_tiles),
        grid=(b, nt),
        in_specs=[tok(d),
                  pl.BlockSpec((1, 1, 6, d), lambda bi, i: (bi, jnp.where(i < n_lat_tiles, 1, 0), 0, 0)),
                  tok(BRANCH_W), tok(BRANCH_W), lat(BRANCH_W), ctx(BRANCH_W), lat(BRANCH_W), ctx(BRANCH_W),
                  tok(N_BRANCH * d),
                  const(wbr.shape), const(wout.shape), const((1, d)), const(wr.shape), const((1, LANES))],
        out_specs=[tok(d), pl.BlockSpec((TM * SUB, LANES), lambda bi, i: (bi * nt + i, 0)), tok(LANES)],
        out_shape=[jax.ShapeDtypeStruct((b, n, d), F32), jax.ShapeDtypeStruct((b * n * SUB, LANES), F32),
                   jax.ShapeDtypeStruct((b, n, LANES), F32)],
        compiler_params=_cparams(2),
        name="merge_router",
    )(xx, mod6, ylru, rg, ym_lat, ym_ctx, yg_lat, yg_ctx, mg, wbr, wout, gffn, wr, br)


SUB = 8


def _to_token_tiles(ref, x):
    rows = x.shape[0]
    for j in range(SUB):
        ref[pl.ds(j, rows, stride=SUB), :] = x[:, j * LANES:(j + 1) * LANES]


def _from_token_tiles(ref, row0, rows):
    return jnp.concatenate([ref[pl.ds(row0 * SUB + j, rows, stride=SUB), :] for j in range(SUB)], axis=1)


def _start_row_gather(idx_ref, n_rows, src_hbm, dst, sem):
    def body(rb, carry):
        for u in range(SUB):
            tok = idx_ref[0, 0, rb * SUB + u]
            pltpu.make_async_copy(src_hbm.at[pl.ds(pl.multiple_of(tok * SUB, SUB), SUB)],
                                  dst.at[pl.ds(pl.multiple_of(rb * (SUB * SUB) + u * SUB, SUB), SUB)], sem).start()
        return carry
    lax.fori_loop(0, n_rows // SUB, body, 0)


def _wait_row_gather(n_rows, src_hbm, dst, sem):
    def body(r, carry):
        pltpu.make_async_copy(src_hbm.at[pl.ds(0, SUB)], dst.at[pl.ds(pl.multiple_of(r * SUB, SUB), SUB)], sem).wait()
        return carry
    lax.fori_loop(0, n_rows, body, 0, unroll=8)


def _expert_kernel(te_ref, nv_ref, rt_cur, rt_nxt, hf_hbm, w1_ref, w3_ref, w2_ref, y_ref, xbuf, sem):
    s = pl.program_id(0)
    nv = nv_ref[0]
    slot = s % 2
    m = xbuf.shape[1] // SUB

    @pl.when(jnp.logical_and(s == 0, nv > 0))
    def _():
        _start_row_gather(rt_cur, m, hf_hbm, xbuf.at[0], sem.at[0])

    @pl.when(s + 1 < nv)
    def _():
        _start_row_gather(rt_nxt, m, hf_hbm, xbuf.at[1 - slot], sem.at[1 - slot])

    @pl.when(s < nv)
    def _():
        _wait_row_gather(m, hf_hbm, xbuf.at[slot], sem.at[slot])
        x = _from_token_tiles(xbuf.at[slot], 0, m).astype(BF16)
        h1 = jnp.dot(x, w1_ref[0], preferred_element_type=F32)
        h3 = jnp.dot(x, w3_ref[0], preferred_element_type=F32)
        a = (h1 * _sigmoid(h1) * h3).astype(BF16)
        _to_token_tiles(y_ref, jnp.dot(a, w2_ref[0], preferred_element_type=F32))

    @pl.when(s >= nv)
    def _():
        y_ref[...] = jnp.zeros_like(y_ref)


def _expert_call(tile_expert, n_valid, row_token, hf_tiles, w1, w3, w2):
    n_tiles = tile_expert.shape[0]
    d, de = w1.shape[1], w1.shape[2]
    assert d == SUB * LANES
    m = TM
    grid_spec = pltpu.PrefetchScalarGridSpec(
        num_scalar_prefetch=2,
        grid=(n_tiles,),
        in_specs=[pl.BlockSpec((1, 1, m), lambda s, te, nv: (s, 0, 0), memory_space=pltpu.SMEM),
                  pl.BlockSpec((1, 1, m), lambda s, te, nv: (jnp.minimum(s + 1, n_tiles - 1), 0, 0),
                               memory_space=pltpu.SMEM),
                  pl.BlockSpec(memory_space=pl.ANY),
                  pl.BlockSpec((1, d, de), lambda s, te, nv: (te[s], 0, 0)),
                  pl.BlockSpec((1, d, de), lambda s, te, nv: (te[s], 0, 0)),
                  pl.BlockSpec((1, de, d), lambda s, te, nv: (te[s], 0, 0))],
        out_specs=pl.BlockSpec((m * SUB, LANES), lambda s, te, nv: (s, 0)),
        scratch_shapes=[pltpu.VMEM((2, m * SUB, LANES), F32), pltpu.SemaphoreType.DMA((2,))],
    )
    return pl.pallas_call(
        _expert_kernel,
        grid_spec=grid_spec,
        out_shape=jax.ShapeDtypeStruct((n_tiles * m * SUB, LANES), F32),
        compiler_params=_cparams(1),
        name="moe_experts",
    )(tile_expert, n_valid, row_token, row_token, hf_tiles, w1, w3, w2)


def _combine_kernel(pos_cur, pos_nxt, x1_ref, mod_ref, ri_ref, gfin_ref, y_hbm, o_ref, ybuf, sem, *, final):
    bi, i = pl.program_id(0), pl.program_id(1)
    nb, ni = pl.num_programs(0), pl.num_programs(1)
    step = bi * ni + i
    slot = step % 2
    rows = ybuf.shape[1] // SUB

    @pl.when(step == 0)
    def _():
        _start_row_gather(pos_cur, rows, y_hbm, ybuf.at[0], sem.at[0])

    @pl.when(step + 1 < nb * ni)
    def _():
        _start_row_gather(pos_nxt, rows, y_hbm, ybuf.at[1 - slot], sem.at[1 - slot])

    _wait_row_gather(rows, y_hbm, ybuf.at[slot], sem.at[slot])
    g_f = mod_ref[0, 0][5:6]
    ri = ri_ref[0]
    p1, p2 = ri[:, 2:3], ri[:, 3:4]
    half = rows // 2
    y = p1 * _from_token_tiles(ybuf.at[slot], 0, half) + p2 * _from_token_tiles(ybuf.at[slot], half, half)
    x2 = x1_ref[0] + g_f * y
    o_ref[0] = _rms(x2, gfin_ref[...]) if final else x2


def _combine_call(pos, x1, mod6, rinfo, gfin, ysorted, n_lat_tiles, final):
    b, n, d = x1.shape
    nt = n // TM
    ni = n_lat_tiles if final else nt

    def pos_tile(step):
        return (step // ni) * nt + step % ni

    def tok(width):
        return pl.BlockSpec((1, TM, width), lambda bi, i: (bi, i, 0))

    kern = functools.partial(_combine_kernel, final=final)
    return pl.pallas_call(
        kern,
        grid=(b, ni),
        in_specs=[pl.BlockSpec((1, 1, 2 * TM), lambda bi, i: (pos_tile(bi * ni + i), 0, 0), memory_space=pltpu.SMEM),
                  pl.BlockSpec((1, 1, 2 * TM),
                               lambda bi, i: (pos_tile(jnp.minimum(bi * ni + i + 1, b * ni - 1)), 0, 0),
                               memory_space=pltpu.SMEM),
                  tok(d),
                  pl.BlockSpec((1, 1, 6, d), lambda bi, i: (bi, jnp.where(i < n_lat_tiles, 1, 0), 0, 0)),
                  tok(LANES),
                  pl.BlockSpec((1, d), lambda bi, i: (0, 0)),
                  pl.BlockSpec(memory_space=pl.ANY)],
        out_specs=tok(d),
        out_shape=jax.ShapeDtypeStruct((b, ni * TM, d), F32),
        scratch_shapes=[pltpu.VMEM((2, 2 * TM * SUB, LANES), F32), pltpu.SemaphoreType.DMA((2,))],
        compiler_params=_cparams(2),
        name="moe_combine_final" if final else "moe_combine",
    )(pos, pos, x1, mod6, rinfo, gfin, ysorted)


def _route(rinfo, b, n):
    t = b * n
    m = TM
    e = rinfo[..., :2].astype(jnp.int32).reshape(t, 2)
    ef = e.T.reshape(-1)
    oh = (ef[:, None] == jnp.arange(N_EXPERTS, dtype=jnp.int32)[None, :]).astype(jnp.int32)
    csum = jnp.cumsum(oh, axis=0)
    rank = jnp.take_along_axis(csum, ef[:, None], axis=1)[:, 0] - 1
    counts = csum[-1]
    padded = ((counts + m - 1) // m) * m
    ends = jnp.cumsum(padded)
    pos = (ends - padded)[ef] + rank
    n_tiles = (2 * t) // m + N_EXPERTS
    n_valid = (ends[-1] // m).astype(jnp.int32).reshape(1)
    tile_start = jnp.arange(n_tiles, dtype=jnp.int32) * m
    tile_expert = jnp.minimum(jnp.sum((tile_start[:, None] >= ends[None, :]).astype(jnp.int32), axis=1),
                              N_EXPERTS - 1).astype(jnp.int32)
    row_token = jnp.zeros((n_tiles * m,), jnp.int32).at[pos].set(jnp.arange(2 * t, dtype=jnp.int32) % t)
    pos_tiles = pos.reshape(2, t // m, 1, m).transpose(1, 2, 0, 3).reshape(t // m, 1, 2 * m)
    return tile_expert, n_valid, row_token.reshape(n_tiles, 1, m), pos_tiles


def _prep_w_in(w_in):
    d = w_in.shape[0]
    o = np.cumsum([0, LRU_WIDTH, MLA_KV_RANK, MLA_ROPE, GQA_KV_HEADS * GQA_DIM, GQA_KV_HEADS * GQA_DIM,
                   LRU_WIDTH, MLA_Q_RANK, GQA_HEADS * GQA_DIM, N_BRANCH * d])
    xr, ckv, kr, gk, gv, rg, cq, gq, mg = [w_in[:, int(o[i]):int(o[i + 1])] for i in range(9)]
    z = lambda w: jnp.zeros((d, w), w_in.dtype)
    return jnp.concatenate([xr, rg, mg, gq, cq, ckv, z(ROPE_LANE0), kr, z(LANES - ROPE_LANE0 - MLA_ROPE), gk, gv],
                           axis=1).astype(BF16)


def _prep_wuq(wuq):
    r = wuq.shape[0]
    w = wuq.reshape(r, MLA_HEADS, MLA_NOPE + MLA_ROPE)
    w = jnp.pad(w, ((0, 0), (0, 0), (0, MLA_SLOT - MLA_NOPE - MLA_ROPE)))
    return w.reshape(r, MLA_HEADS * MLA_SLOT).astype(BF16)


def _prep_wukv(wukv):
    r = wukv.shape[0]
    w = wukv.reshape(r, MLA_HEADS, MLA_NOPE + MLA_V)
    k = jnp.pad(w[:, :, :MLA_NOPE], ((0, 0), (0, 0), (0, MLA_SLOT - MLA_NOPE))).reshape(r, MLA_HEADS * MLA_SLOT)
    v = w[:, :, MLA_NOPE:].reshape(r, MLA_HEADS * MLA_V)
    return jnp.concatenate([k, v], axis=1).astype(BF16)


def _place_matrix():
    p = np.zeros((LANES, MLA_HEADS * MLA_SLOT), np.float32)
    for hh in range(MLA_HEADS):
        for r in range(MLA_ROPE):
            p[ROPE_LANE0 + r, hh * MLA_SLOT + ROPE_LANE0 + r] = 1.0
    return jnp.asarray(p, BF16)


def _kv_select_matrices():
    kvw = GQA_KV_HEADS * GQA_DIM
    sk = np.zeros((GQA_KV_HEADS, kvw, LANES), np.float32)
    sv = np.zeros((GQA_KV_HEADS, kvw, LANES), np.float32)
    for g in range(GQA_KV_HEADS):
        for c in range(GQA_DIM):
            sk[g, g * GQA_DIM + c, c] = 1.0
            sv[g, g * GQA_DIM + c, c] = 1.0
    return jnp.asarray(sk, BF16), jnp.asarray(sv.transpose(0, 2, 1), BF16)


def _block_diag_pairs(w):
    per = LANES // LRU_BLOCK
    nd = w.shape[0]
    w = w.reshape(nd, LRU_BLOCKS // per, per, LRU_BLOCK, LRU_BLOCK)
    eye = jnp.eye(per, dtype=w.dtype)
    bd = jnp.einsum("dgpij,pq->dgpiqj", w, eye)
    return bd.reshape(nd, LRU_BLOCKS // per, LANES, LANES)


def kernel(x, c, ctx, c_ctx, w_mod, b_mod, g_mix, g_ffn, w_in, conv_w, conv_b, lru_wa, lru_ba, lru_wi, lru_bi,
           lru_lambda, mla_gq, mla_wuq, mla_gkv, mla_wukv, gqa_gq, gqa_gk, w_branch, w_out, moe_wg, moe_bg,
           moe_we, moe_be, moe_w1, moe_w3, moe_w2, g_final):
    b, seq, d = x.shape
    n_ctx = ctx.shape[1]
    n = n_ctx + seq
    depth = w_mod.shape[0]
    assert n_ctx == TM and seq % KC == 0 and seq % GRID_W == 0 and d == 1024
    n_lat_tiles = seq // TM

    xx = jnp.concatenate([x, ctx], axis=1)
    tab_m, tab_g = _rope_tables(n_ctx, seq)
    place = _place_matrix()
    selk, selv = _kv_select_matrices()
    mod_rows = 16
    cc = jnp.concatenate([c, c_ctx[None, :], jnp.zeros((mod_rows - b - 1, d), F32)], axis=0)

    out = None
    for l in range(depth):
        mod = _mod_call(cc, w_mod[l], b_mod[l])
        mod6 = jnp.stack([jnp.broadcast_to(mod[b].reshape(1, 6, d), (b, 6, d)), mod[:b].reshape(b, 6, d)], axis=1)

        xr, rg, mg, qm, km, vm, qg, kg, vg = _inproj_call(
            xx, mod6, g_mix[l].reshape(1, d), _prep_w_in(w_in[l]), _prep_wuq(mla_wuq[l]), _prep_wukv(mla_wukv[l]),
            place, mla_gq[l].reshape(1, -1), mla_gkv[l].reshape(1, -1),
            jnp.tile(gqa_gq[l], LANES // GQA_DIM).reshape(1, LANES),
            jnp.tile(gqa_gk[l], LANES // GQA_DIM).reshape(1, LANES), tab_m, tab_g, n_lat_tiles)

        xr_t = jnp.transpose(xr, (1, 0, 2))
        wa_bd, wi_bd = _block_diag_pairs(lru_wa[l]), _block_diag_pairs(lru_wi[l])
        lru_args = lambda dr: (conv_w[l], conv_b[l].reshape(1, -1), wa_bd[dr], wi_bd[dr],
                               lru_ba[l, dr].reshape(1, -1), lru_bi[l, dr].reshape(1, -1),
                               lru_lambda[l, dr].reshape(1, -1))
        y_f = _lru_call(xr_t, None, *lru_args(0), n_ctx, False)
        y_t = _lru_call(xr_t, y_f, *lru_args(1), n_ctx, True)
        ylru = jnp.transpose(y_t, (1, 0, 2))

        ym_lat = _mla_attn_call(qm, km, vm, seq, n_ctx, False)
        ym_ctx = _mla_attn_call(qm, km, vm, seq, n_ctx, True)
        yg_lat = _gqa_attn_call(qg, kg, vg, selk, selv, seq, n_ctx, False)
        yg_ctx = _gqa_attn_call(qg, kg, vg, selk, selv, seq, n_ctx, True)

        wr = jnp.concatenate([moe_wg[l], moe_we[l], jnp.zeros((d, LANES - N_GROUPS - N_EXPERTS), F32)], axis=1)
        wr_hi = wr.astype(BF16)
        wr = jnp.stack([wr_hi, (wr - wr_hi.astype(F32)).astype(BF16)])
        br = jnp.concatenate([moe_bg[l], moe_be[l], jnp.zeros((LANES - N_GROUPS - N_EXPERTS,), F32)]).reshape(1, LANES)
        x1, hf, rinfo = _merge_call(xx, mod6, ylru, rg, ym_lat, ym_ctx, yg_lat, yg_ctx, mg,
                                    w_branch[l].astype(BF16), w_out[l].astype(BF16),
                                    g_ffn[l].reshape(1, d), wr, br, n_lat_tiles)

        tile_expert, n_valid, row_token, pos_tiles = _route(rinfo, b, n)
        ysorted = _expert_call(tile_expert, n_valid, row_token, hf,
                               moe_w1[l].astype(BF16), moe_w3[l].astype(BF16), moe_w2[l].astype(BF16))
        final = l == depth - 1
        out = _combine_call(pos_tiles, x1, mod6, rinfo, g_final.reshape(1, d), ysorted, n_lat_tiles, final)
        xx = out
    return out
```

```python
import functools

import numpy as np
import jax
import jax.numpy as jnp
from jax import lax
from jax.experimental import pallas as pl
from jax.experimental.pallas import tpu as pltpu

F32 = jnp.float32
BF16 = jnp.bfloat16

EPS = 1e-6
ROPE_THETA = 10000.0
GRID_W = 64
LOG2E = 1.4426950408889634

LRU_WIDTH = 512
LRU_BLOCKS = 8
LRU_BLOCK = LRU_WIDTH // LRU_BLOCKS
CONV_W = 4
LRU_C = 8.0
MLA_HEADS = 8
MLA_Q_RANK = 256
MLA_KV_RANK = 128
MLA_NOPE = 64
MLA_ROPE = 32
MLA_V = 64
MLA_SCALE = (MLA_NOPE + MLA_ROPE) ** -0.5
GQA_HEADS = 8
GQA_KV_HEADS = 2
GQA_GROUP = GQA_HEADS // GQA_KV_HEADS
GQA_DIM = 64
GQA_SCALE = GQA_DIM ** -0.5
N_BRANCH = 3
BRANCH_W = 512
N_GROUPS = 4
EXPERTS_PER_GROUP = 8
N_EXPERTS = N_GROUPS * EXPERTS_PER_GROUP
D_EXPERT = 256

LANES = 128
TM = 256
KC = 1024
MLA_RQ = 512
MLA_HP = 4
GQA_RQ = 512
VMEM_LIMIT = 56 * 1024 * 1024

C_XR = 0
C_RG = C_XR + LRU_WIDTH
C_MG = C_RG + LRU_WIDTH
C_GQ = C_MG + N_BRANCH * 1024
C_SMALL = C_GQ + GQA_HEADS * GQA_DIM
S_CQ = 0
S_CKV = S_CQ + MLA_Q_RANK
S_KR = S_CKV + MLA_KV_RANK
S_GK = S_KR + LANES
S_GV = S_GK + GQA_KV_HEADS * GQA_DIM
SMALL_W = S_GV + GQA_KV_HEADS * GQA_DIM
MLA_SLOT = 128
ROPE_LANE0 = MLA_NOPE
ONES_ROW = 64
VT_ROWS = 80
assert MLA_V == ONES_ROW and GQA_DIM == ONES_ROW


def _cparams(n_axes):
    return pltpu.CompilerParams(dimension_semantics=("arbitrary",) * n_axes,
                                vmem_limit_bytes=VMEM_LIMIT)


def _rms(x, g):
    return x * lax.rsqrt(jnp.mean(x * x, axis=-1, keepdims=True) + EPS) * g


def _sigmoid(x):
    return 1.0 / (1.0 + jnp.exp(-x))


def _lane_iota(shape):
    return lax.broadcasted_iota(jnp.int32, shape, len(shape) - 1)


def _mod_kernel(c_ref, w_ref, b_ref, o_ref):
    c = c_ref[...]
    s = c * _sigmoid(c)
    o_ref[...] = jnp.dot(s, w_ref[...], preferred_element_type=F32) + b_ref[...]


def _mod_call(cc, w, b):
    rows, d = cc.shape
    n = w.shape[1]
    bn = 512
    return pl.pallas_call(
        _mod_kernel,
        grid=(n // bn,),
        in_specs=[pl.BlockSpec((rows, d), lambda j: (0, 0)),
                  pl.BlockSpec((d, bn), lambda j: (0, j)),
                  pl.BlockSpec((1, bn), lambda j: (0, j))],
        out_specs=pl.BlockSpec((rows, bn), lambda j: (0, j)),
        out_shape=jax.ShapeDtypeStruct((rows, n), F32),
        compiler_params=_cparams(1),
        name="adaln_mod",
    )(cc, w, b.reshape(1, n))


def _rope_tables(n_ctx, seq):
    rows = seq // GRID_W
    row = np.repeat(np.arange(rows), GRID_W).astype(np.float64)
    col = np.tile(np.arange(GRID_W), rows).astype(np.float64)

    def angles(rot_dim):
        quarter = rot_dim // 4
        freqs = ROPE_THETA ** (-np.arange(quarter, dtype=np.float64) / quarter)
        return np.concatenate([row[:, None] * freqs, col[:, None] * freqs], axis=-1)

    n = n_ctx + seq
    am = angles(MLA_ROPE)
    half = MLA_ROPE // 2
    cm = np.ones((n, LANES)); sm = np.zeros((n, LANES))
    cm[:seq, ROPE_LANE0:ROPE_LANE0 + half] = np.cos(am)
    cm[:seq, ROPE_LANE0 + half:ROPE_LANE0 + 2 * half] = np.cos(am)
    sm[:seq, ROPE_LANE0:ROPE_LANE0 + half] = -np.sin(am)
    sm[:seq, ROPE_LANE0 + half:ROPE_LANE0 + 2 * half] = np.sin(am)
    ag = angles(GQA_DIM)
    hg = GQA_DIM // 2
    cg = np.ones((n, LANES)); sg = np.zeros((n, LANES))
    for h0 in (0, GQA_DIM):
        cg[:seq, h0:h0 + hg] = np.cos(ag)
        cg[:seq, h0 + hg:h0 + 2 * hg] = np.cos(ag)
        sg[:seq, h0:h0 + hg] = -np.sin(ag)
        sg[:seq, h0 + hg:h0 + 2 * hg] = np.sin(ag)
    tab_m = np.concatenate([cm, sm], axis=-1).astype(np.float32)
    tab_g = np.concatenate([cg, sg], axis=-1).astype(np.float32)
    return jnp.asarray(tab_m), jnp.asarray(tab_g)


def _rope_mla(y, cos, sin):
    half = MLA_ROPE // 2
    lane = _lane_iota(y.shape)
    rot = jnp.where(lane < ROPE_LANE0 + half,
                    pltpu.roll(y, LANES - half, 1),
                    pltpu.roll(y, half, 1))
    return y * cos + rot * sin


def _rope_gqa(y, cos, sin):
    half = GQA_DIM // 2
    lane = _lane_iota(y.shape)
    rot = jnp.where((lane % GQA_DIM) < half,
                    pltpu.roll(y, LANES - half, 1),
                    pltpu.roll(y, half, 1))
    return y * cos + rot * sin


def _head_rms_pair(t, g):
    lane = _lane_iota(t.shape)
    lo = lane < GQA_DIM
    t2 = t * t
    s_lo = jnp.sum(jnp.where(lo, t2, 0.0), axis=-1, keepdims=True)
    s_hi = jnp.sum(jnp.where(lo, 0.0, t2), axis=-1, keepdims=True)
    r = jnp.where(lo, lax.rsqrt(s_lo * (1.0 / GQA_DIM) + EPS), lax.rsqrt(s_hi * (1.0 / GQA_DIM) + EPS))
    return t * r * g


def _inproj_kernel(x_ref, mod_ref, gmix_ref, w_ref, wuq_ref, wukv_ref, place_ref,
                   gq_ref, gkv_ref, ggq_ref, ggk_ref, tabm_ref, tabg_ref,
                   xr_ref, rg_ref, mg_ref, qm_ref, km_ref, vm_ref, qg_ref, kg_ref, vg_ref):
    x = x_ref[0]
    m6 = mod_ref[0, 0]
    sh, sc = m6[0:1], m6[1:2]
    h = (_rms(x, gmix_ref[...]) * (1.0 + sc) + sh).astype(BF16)

    def proj(c0, width):
        return jnp.dot(h, w_ref[:, c0:c0 + width], preferred_element_type=F32)

    xr_ref[0] = proj(C_XR, LRU_WIDTH)
    rg_ref[0] = proj(C_RG, LRU_WIDTH).astype(BF16)
    for j in range(N_BRANCH * 1024 // 512):
        mg_ref[0, :, j * 512:(j + 1) * 512] = proj(C_MG + j * 512, 512).astype(BF16)

    cos_m, sin_m = tabm_ref[:, :LANES], tabm_ref[:, LANES:]
    cos_g, sin_g = tabg_ref[:, :LANES], tabg_ref[:, LANES:]
    lane = _lane_iota((TM, LANES))

    gq = proj(C_GQ, GQA_HEADS * GQA_DIM)
    for v in range(GQA_HEADS * GQA_DIM // LANES):
        t = gq[:, v * LANES:(v + 1) * LANES]
        y = _rope_gqa(_head_rms_pair(t, ggq_ref[...]), cos_g, sin_g) * (GQA_SCALE * LOG2E)
        ysw = pltpu.roll(y, GQA_DIM, 1)
        qg_ref[0, :, (2 * v) * LANES:(2 * v + 1) * LANES] = jnp.where(lane < GQA_DIM, y, ysw).astype(BF16)
        qg_ref[0, :, (2 * v + 1) * LANES:(2 * v + 2) * LANES] = jnp.where(lane < GQA_DIM, ysw, y).astype(BF16)

    small = proj(C_SMALL, SMALL_W)
    cq = small[:, S_CQ:S_CQ + MLA_Q_RANK]
    ckv = small[:, S_CKV:S_CKV + MLA_KV_RANK]
    krp = small[:, S_KR:S_KR + LANES]
    gk = small[:, S_GK:S_GK + LANES]
    gv = small[:, S_GV:S_GV + LANES]

    kg_ref[0] = _rope_gqa(_head_rms_pair(gk, ggk_ref[...]), cos_g, sin_g).astype(BF16)
    vg_ref[0] = gv.astype(BF16)

    cqn = _rms(cq, gq_ref[...]).astype(BF16)
    qu = jnp.dot(cqn, wuq_ref[...], preferred_element_type=F32)
    for hh in range(MLA_HEADS):
        y = _rope_mla(qu[:, hh * MLA_SLOT:(hh + 1) * MLA_SLOT], cos_m, sin_m)
        qm_ref[0, :, hh * MLA_SLOT:(hh + 1) * MLA_SLOT] = (y * (MLA_SCALE * LOG2E)).astype(BF16)

    ckvn = _rms(ckv, gkv_ref[...]).astype(BF16)
    kvu = jnp.dot(ckvn, wukv_ref[...], preferred_element_type=F32)
    kr = _rope_mla(krp, cos_m, sin_m).astype(BF16)
    kr_placed = jnp.dot(kr, place_ref[...], preferred_element_type=F32)
    n_k = MLA_HEADS * MLA_SLOT
    km_ref[0] = (kvu[:, :n_k] + kr_placed).astype(BF16)
    vm_ref[0] = kvu[:, n_k:].astype(BF16)


def _inproj_call(xx, mod6, gmix, w_all, wuq, wukv, place, gq, gkv, ggq, ggk, tab_m, tab_g, n_lat_tiles):
    b, n, d = xx.shape
    nt = n // TM
    ncol = w_all.shape[1]

    def tok(width):
        return pl.BlockSpec((1, TM, width), lambda bi, i: (bi, i, 0))

    def const(shape):
        return pl.BlockSpec(shape, lambda bi, i: (0,) * len(shape))

    out_w = [(LRU_WIDTH, F32), (LRU_WIDTH, BF16), (N_BRANCH * 1024, BF16),
             (MLA_HEADS * MLA_SLOT, BF16), (MLA_HEADS * MLA_SLOT, BF16), (MLA_HEADS * MLA_V, BF16),
             (GQA_HEADS * LANES, BF16), (GQA_KV_HEADS * GQA_DIM, BF16), (GQA_KV_HEADS * GQA_DIM, BF16)]
    return pl.pallas_call(
        _inproj_kernel,
        grid=(b, nt),
        in_specs=[tok(d),
                  pl.BlockSpec((1, 1, 6, d), lambda bi, i: (bi, jnp.where(i < n_lat_tiles, 1, 0), 0, 0)),
                  const((1, d)), const((d, ncol)), const(wuq.shape), const(wukv.shape), const(place.shape),
                  const((1, MLA_Q_RANK)), const((1, MLA_KV_RANK)), const((1, LANES)), const((1, LANES)),
                  pl.BlockSpec((TM, 2 * LANES), lambda bi, i: (i, 0)),
                  pl.BlockSpec((TM, 2 * LANES), lambda bi, i: (i, 0))],
        out_specs=[tok(w) for w, _ in out_w],
        out_shape=[jax.ShapeDtypeStruct((b, n, w), dt) for w, dt in out_w],
        compiler_params=_cparams(2),
        name="in_proj",
    )(xx, mod6, gmix, w_all, wuq, wukv, place, gq, gkv, ggq, ggk, tab_m, tab_g)


def _flash_t(streams, n_main, tail_start, tail_size):
    nt_dims = (((1,), (1,)), ((), ()))
    ppc = KC // TM

    def scores(start, size):
        return tuple(lax.dot_general(k_fn(start, size), q, nt_dims, preferred_element_type=F32)
                     for q, k_fn, _ in streams)

    def softmax_pv(sts, size, piece0, carries):
        stats = []
        for st, (m, _) in zip(sts, carries):
            m_new = jnp.maximum(m, jnp.max(st, axis=0, keepdims=True))
            stats.append((m_new, jnp.exp2(m - m_new), jnp.exp2(st - m_new).astype(BF16)))
        out = []
        for (_, _, vt_fn), (m_new, alpha, pb), (_, acc) in zip(streams, stats, carries):
            pv = None
            for t in range(size // TM):
                part = jnp.dot(vt_fn(piece0 + t), pb[t * TM:(t + 1) * TM], preferred_element_type=F32)
                pv = part if pv is None else pv + part
            out.append((m_new, alpha * acc + pv))
        return tuple(out)

    carries = tuple((jnp.full((1, q.shape[0]), -jnp.inf, F32), jnp.zeros((VT_ROWS, q.shape[0]), F32))
                    for q, _, _ in streams)
    chunks = [(j * KC, KC) for j in range(n_main)] + [(tail_start, tail_size)]
    sts = scores(*chunks[0])
    for idx, (start, size) in enumerate(chunks):
        nxt = scores(*chunks[idx + 1]) if idx + 1 < len(chunks) else None
        carries = softmax_pv(sts, size, start // TM, carries)
        sts = nxt
    return [acc[:ONES_ROW] * (1.0 / acc[ONES_ROW:ONES_ROW + 1]) for _, acc in carries]


def _values_t(sel, v, row):
    vt = lax.dot_general(sel, v, (((1,), (1,)), ((), ())), preferred_element_type=F32)
    return jnp.where(row == ONES_ROW, 1.0, vt).astype(BF16)


def _mla_attn_kernel(q_ref, k_ref, v_ref, o_ref, vt, *, hp, n_main, tail_start, tail_size):
    nk = k_ref.shape[1]

    @pl.when(pl.program_id(2) == 0)
    def _():
        row = lax.broadcasted_iota(jnp.int32, (VT_ROWS, TM), 0)
        r_i = lax.broadcasted_iota(jnp.int32, (VT_ROWS, LANES), 0)
        c_i = _lane_iota((VT_ROWS, LANES))
        for hh in range(hp):
            sel = jnp.where(jnp.logical_and(r_i < MLA_V, c_i == r_i + (hh % 2) * MLA_V), 1.0, 0.0).astype(BF16)
            for c in range(nk // TM):
                v = v_ref[0, c * TM:(c + 1) * TM, (hh // 2) * LANES:(hh // 2 + 1) * LANES]
                vt[hh, c] = _values_t(sel, v, row)

    streams = []
    for hh in range(hp):
        q = q_ref[0, :, hh * MLA_SLOT:(hh + 1) * MLA_SLOT]
        k_fn = functools.partial(lambda st, sz, c0: k_ref[0, pl.ds(st, sz), c0:c0 + MLA_SLOT], c0=hh * MLA_SLOT)
        vt_fn = functools.partial(lambda piece, h: vt[h, piece], h=hh)
        streams.append((q, k_fn, vt_fn))
    outs = _flash_t(streams, n_main, tail_start, tail_size)
    for pr in range(hp // 2):
        o_t = jnp.concatenate([outs[2 * pr], outs[2 * pr + 1]], axis=0)
        o_ref[0, :, pr * LANES:(pr + 1) * LANES] = o_t.T.astype(BF16)


def _mla_attn_call(qm, km, vm, seq, n_ctx, ctx_queries):
    b = qm.shape[0]
    hp = MLA_HP
    if ctx_queries:
        rq, nq, q0, nk, k0 = n_ctx, 1, seq // n_ctx, n_ctx, seq // n_ctx
        plan = dict(n_main=0, tail_start=0, tail_size=n_ctx)
    else:
        rq, nq, q0, nk, k0 = MLA_RQ, seq // MLA_RQ, 0, seq + n_ctx, 0
        plan = dict(n_main=seq // KC - 1, tail_start=seq - KC, tail_size=KC + n_ctx)
    kern = functools.partial(_mla_attn_kernel, hp=hp, **plan)
    return pl.pallas_call(
        kern,
        grid=(b, MLA_HEADS // hp, nq),
        in_specs=[pl.BlockSpec((1, rq, hp * MLA_SLOT), lambda bi, p, i: (bi, i + q0, p)),
                  pl.BlockSpec((1, nk, hp * MLA_SLOT), lambda bi, p, i: (bi, k0, p)),
                  pl.BlockSpec((1, nk, hp * MLA_V), lambda bi, p, i: (bi, k0, p))],
        out_specs=pl.BlockSpec((1, rq, hp * MLA_V), lambda bi, p, i: (bi, i, p)),
        out_shape=jax.ShapeDtypeStruct((b, rq * nq, MLA_HEADS * MLA_V), BF16),
        scratch_shapes=[pltpu.VMEM((hp, nk // TM, VT_ROWS, TM), BF16)],
        compiler_params=_cparams(3),
        name="mla_attention_ctx" if ctx_queries else "mla_attention",
    )(qm, km, vm)


def _gqa_attn_kernel(q_ref, k_ref, v_ref, selk_ref, selvt_ref, o_ref, kd, vt, *, n_main, tail_start, tail_size):
    i = pl.program_id(2)
    nk = kd.shape[0]
    rq = q_ref.shape[1]

    @pl.when(i == 0)
    def _():
        row = lax.broadcasted_iota(jnp.int32, (VT_ROWS, TM), 0)
        for c in range(nk // TM):
            rows = pl.ds(c * TM, TM)
            kd[rows, :] = jnp.dot(k_ref[0, rows, :], selk_ref[0], preferred_element_type=F32).astype(BF16)
            vt[c] = _values_t(selvt_ref[0], v_ref[0, rows, :], row)

    streams = [(q_ref[0, :, j * LANES:(j + 1) * LANES], lambda st, sz: kd[pl.ds(st, sz), :], lambda piece: vt[piece])
               for j in range(GQA_GROUP)]
    outs = _flash_t(streams, n_main, tail_start, tail_size)
    for pr in range(GQA_GROUP // 2):
        o_t = jnp.concatenate([outs[2 * pr], outs[2 * pr + 1]], axis=0)
        o_ref[0, :, pr * LANES:(pr + 1) * LANES] = o_t.T.astype(BF16)


def _gqa_attn_call(qg, kg, vg, selk, selv, seq, n_ctx, ctx_queries):
    b = qg.shape[0]
    gw = GQA_GROUP * GQA_DIM
    kvw = GQA_KV_HEADS * GQA_DIM
    if ctx_queries:
        rq, nq, q0, nk, k0 = n_ctx, 1, seq // n_ctx, n_ctx, seq // n_ctx
        plan = dict(n_main=0, tail_start=0, tail_size=n_ctx)
    else:
        rq, nq, q0, nk, k0 = GQA_RQ, seq // GQA_RQ, 0, seq + n_ctx, 0
        plan = dict(n_main=seq // KC - 1, tail_start=seq - KC, tail_size=KC + n_ctx)
    kern = functools.partial(_gqa_attn_kernel, **plan)
    return pl.pallas_call(
        kern,
        grid=(b, GQA_KV_HEADS, nq),
        in_specs=[pl.BlockSpec((1, rq, GQA_GROUP * LANES), lambda bi, g, i: (bi, i + q0, g)),
                  pl.BlockSpec((1, nk, kvw), lambda bi, g, i: (bi, k0, 0)),
                  pl.BlockSpec((1, nk, kvw), lambda bi, g, i: (bi, k0, 0)),
                  pl.BlockSpec((1, kvw, LANES), lambda bi, g, i: (g, 0, 0)),
                  pl.BlockSpec((1, VT_ROWS, kvw), lambda bi, g, i: (g, 0, 0))],
        out_specs=pl.BlockSpec((1, rq, gw), lambda bi, g, i: (bi, i, g)),
        out_shape=jax.ShapeDtypeStruct((b, rq * nq, GQA_HEADS * GQA_DIM), BF16),
        scratch_shapes=[pltpu.VMEM((nk, LANES), BF16), pltpu.VMEM((nk // TM, VT_ROWS, TM), BF16)],
        compiler_params=_cparams(3),
        name="gqa_attention_ctx" if ctx_queries else "gqa_attention",
    )(qg, kg, vg, selk, selv)


def _softplus(z):
    return jnp.maximum(z, 0.0) + jnp.log1p(jnp.exp(-jnp.abs(z)))


def _lru_chunk(j, reverse, n_lat_chunks):
    return jnp.where(j == 0, n_lat_chunks, n_lat_chunks - j if reverse else j - 1)


def _lru_kernel(*refs, reverse, n_chunks, n_lat_chunks):
    if reverse:
        (x_ref, hp_ref, hn_ref, cw_ref, cb_ref, wa_ref, wi_ref, ba_ref, bi_ref, lam_ref, yin_ref,
         y_ref, xs, a_s, b_s, h_s) = refs
    else:
        (x_ref, hp_ref, hn_ref, cw_ref, cb_ref, wa_ref, wi_ref, ba_ref, bi_ref, lam_ref,
         y_ref, xs, a_s, b_s, h_s) = refs
        yin_ref = None
    j = pl.program_id(1)
    c = _lru_chunk(j, reverse, n_lat_chunks)
    ch, bsz, lw = x_ref.shape

    @pl.when(j == 0)
    def _():
        h_s[...] = jnp.zeros_like(h_s)

    seq_first = jnp.logical_or(c == 0, c == n_lat_chunks)
    seq_last = jnp.logical_or(c == n_lat_chunks - 1, c == n_chunks - 1)
    xs[0:2] = jnp.where(seq_first, 0.0, hp_ref[...])
    xs[2:2 + ch] = x_ref[...]
    xs[2 + ch:3 + ch] = jnp.where(seq_last, 0.0, hn_ref[...])
    xc = cb_ref[...]
    for tap in range(CONV_W):
        xc = xc + cw_ref[tap:tap + 1, :] * xs[tap:tap + ch]
    x2 = xc.reshape(ch * bsz, lw)
    r = _sigmoid(jnp.dot(x2, wa_ref[0], preferred_element_type=F32) + ba_ref[...])
    gi = _sigmoid(jnp.dot(x2, wi_ref[0], preferred_element_type=F32) + bi_ref[...])
    log_a = (-LRU_C) * r * _softplus(-lam_ref[...])
    a = jnp.exp(log_a)
    a_s[...] = a.reshape(ch, bsz, lw)
    mult = jnp.sqrt(-jnp.tanh(log_a) * (a * a + 1.0))
    b_s[...] = (mult * gi * x2).reshape(ch, bsz, lw)

    def body(tt, h):
        t = ch - 1 - tt if reverse else tt
        h = a_s[t] * h + b_s[t]
        y_ref[t] = h + yin_ref[t] if reverse else h
        return h

    h_s[...] = lax.fori_loop(0, ch, body, h_s[...], unroll=8)


def _lru_call(xr_t, yin, cw, cb, wa_bd, wi_bd, ba, bi, lam, n_ctx, reverse):
    n, bsz, width = xr_t.shape
    ch = TM
    n_chunks = n // ch
    n_lat_chunks = (n - n_ctx) // ch
    groups = width // LANES
    cmap = functools.partial(_lru_chunk, reverse=reverse, n_lat_chunks=n_lat_chunks)

    main = pl.BlockSpec((ch, bsz, LANES), lambda g, j: (cmap(j), 0, g))
    in_specs = [main,
                pl.BlockSpec((2, bsz, LANES), lambda g, j: (jnp.maximum(cmap(j) * (ch // 2) - 1, 0), 0, g)),
                pl.BlockSpec((1, bsz, LANES), lambda g, j: (jnp.minimum((cmap(j) + 1) * ch, n - 1), 0, g)),
                pl.BlockSpec((CONV_W, LANES), lambda g, j: (0, g)),
                pl.BlockSpec((1, LANES), lambda g, j: (0, g)),
                pl.BlockSpec((1, LANES, LANES), lambda g, j: (g, 0, 0)),
                pl.BlockSpec((1, LANES, LANES), lambda g, j: (g, 0, 0)),
                pl.BlockSpec((1, LANES), lambda g, j: (0, g)),
                pl.BlockSpec((1, LANES), lambda g, j: (0, g)),
                pl.BlockSpec((1, LANES), lambda g, j: (0, g))]
    args = [xr_t, xr_t, xr_t, cw, cb, wa_bd, wi_bd, ba, bi, lam]
    if reverse:
        in_specs.append(main)
        args.append(yin)
    kern = functools.partial(_lru_kernel, reverse=reverse, n_chunks=n_chunks, n_lat_chunks=n_lat_chunks)
    return pl.pallas_call(
        kern,
        grid=(groups, n_chunks),
        in_specs=in_specs,
        out_specs=main,
        out_shape=jax.ShapeDtypeStruct((n, bsz, width), F32),
        scratch_shapes=[pltpu.VMEM((ch + 3, bsz, LANES), F32), pltpu.VMEM((ch, bsz, LANES), F32),
                        pltpu.VMEM((ch, bsz, LANES), F32), pltpu.VMEM((bsz, LANES), F32)],
        compiler_params=_cparams(2),
        name="rglru_rev" if reverse else "rglru_fwd",
    )(*args)


def _gelu_tanh(x):
    return 0.5 * x * (1.0 + jnp.tanh(0.7978845608028654 * (x + 0.044715 * (x * x * x))))


def _merge_kernel(x_ref, mod_ref, ylru_ref, rg_ref, yml_ref, ymc_ref, ygl_ref, ygc_ref, mg_ref, wbr_ref, wout_ref,
                  gffn_ref, wr_ref, br_ref, x1_ref, hf_ref, ri_ref, *, n_lat_tiles):
    d = x_ref.shape[-1]
    is_lat = pl.program_id(1) < n_lat_tiles
    m6 = mod_ref[0, 0]
    g_a, sh_f, sc_f = m6[2:3], m6[3:4], m6[4:5]
    y_rnn = (ylru_ref[0] * _gelu_tanh(rg_ref[0].astype(F32))).astype(BF16)
    y_mla = jnp.where(is_lat, yml_ref[0], ymc_ref[0])
    y_gqa = jnp.where(is_lat, ygl_ref[0], ygc_ref[0])
    acc = jnp.zeros((TM, d), F32)
    for kbr, br in enumerate((y_rnn, y_mla, y_gqa)):
        pr = jnp.dot(br, wbr_ref[kbr], preferred_element_type=F32)
        acc = acc + _sigmoid(mg_ref[0, :, kbr * d:(kbr + 1) * d].astype(F32)) * pr
    out = jnp.dot(acc.astype(BF16), wout_ref[...], preferred_element_type=F32)
    x1 = x_ref[0] + g_a * out
    x1_ref[0] = x1
    hf = _rms(x1, gffn_ref[...]) * (1.0 + sc_f) + sh_f
    _to_token_tiles(hf_ref, hf)

    hf_hi = hf.astype(BF16)
    hf_lo = (hf - hf_hi.astype(F32)).astype(BF16)
    lg = (jnp.dot(hf_hi, wr_ref[0], preferred_element_type=F32)
          + jnp.dot(hf_lo, wr_ref[0], preferred_element_type=F32)
          + jnp.dot(hf_hi, wr_ref[1], preferred_element_type=F32)) + br_ref[...]
    lane = _lane_iota(lg.shape)
    big = jnp.int32(1 << 20)
    neg = -jnp.inf
    is_g = lane < N_GROUPS
    gl = jnp.where(is_g, lg, neg)
    gmax = jnp.max(gl, axis=-1, keepdims=True)
    gsel = jnp.min(jnp.where(gl == gmax, lane, big), axis=-1, keepdims=True)
    pg = 1.0 / jnp.sum(jnp.where(is_g, jnp.exp(lg - gmax), 0.0), axis=-1, keepdims=True)
    e0 = N_GROUPS + gsel * EXPERTS_PER_GROUP
    el = jnp.where(jnp.logical_and(lane >= e0, lane < e0 + EXPERTS_PER_GROUP), lg, neg)
    v1 = jnp.max(el, axis=-1, keepdims=True)
    i1 = jnp.min(jnp.where(el == v1, lane, big), axis=-1, keepdims=True)
    el2 = jnp.where(lane == i1, neg, el)
    v2 = jnp.max(el2, axis=-1, keepdims=True)
    i2 = jnp.min(jnp.where(el2 == v2, lane, big), axis=-1, keepdims=True)
    t = jnp.exp(v2 - v1)
    p1 = pg / (1.0 + t)
    p2 = p1 * t
    ri = jnp.where(lane == 0, (i1 - N_GROUPS).astype(F32),
                   jnp.where(lane == 1, (i2 - N_GROUPS).astype(F32),
                             jnp.where(lane == 2, p1, jnp.where(lane == 3, p2, 0.0))))
    ri_ref[0] = ri


def _merge_call(xx, mod6, ylru, rg, ym_lat, ym_ctx, yg_lat, yg_ctx, mg, wbr, wout, gffn, wr, br, n_lat_tiles):
    b, n, d = xx.shape
    nt = n // TM

    def tok(width):
        return pl.BlockSpec((1, TM, width), lambda bi, i: (bi, i, 0))

    def lat(width):
        return pl.BlockSpec((1, TM, width), lambda bi, i: (bi, jnp.minimum(i, n_lat_tiles - 1), 0))

    def ctx(width):
        return pl.BlockSpec((1, TM, width), lambda bi, i: (bi, jnp.maximum(i - n_lat_tiles, 0), 0))

    def const(shape):
        return pl.BlockSpec(shape, lambda bi, i: (0,) * len(shape))

    return pl.pallas_call(
        functools.partial(_merge_kernel, n_lat_tiles=n_lat_tiles),
        grid=(b, nt),
        in_specs=[tok(d),
                  pl.BlockSpec((1, 1, 6, d), lambda bi, i: (bi, jnp.where(i < n_lat_tiles, 1, 0), 0, 0)),
                  tok(BRANCH_W), tok(BRANCH_W), lat(BRANCH_W), ctx(BRANCH_W), lat(BRANCH_W), ctx(BRANCH_W),
                  tok(N_BRANCH * d),
                  const(wbr.shape), const(wout.shape), const((1, d)), const(wr.shape), const((1, LANES))],
        out_specs=[tok(d), pl.BlockSpec((TM * SUB, LANES), lambda bi, i: (bi * nt + i, 0)), tok(LANES)],
        out_shape=[jax.ShapeDtypeStruct((b, n, d), F32), jax.ShapeDtypeStruct((b * n * SUB, LANES), F32),
                   jax.ShapeDtypeStruct((b, n, LANES), F32)],
        compiler_params=_cparams(2),
        name="merge_router",
    )(xx, mod6, ylru, rg, ym_lat, ym_ctx, yg_lat, yg_ctx, mg, wbr, wout, gffn, wr, br)


SUB = 8


def _to_token_tiles(ref, x):
    rows = x.shape[0]
    for j in range(SUB):
        ref[pl.ds(j, rows, stride=SUB), :] = x[:, j * LANES:(j + 1) * LANES]


def _from_token_tiles(ref, row0, rows):
    return jnp.concatenate([ref[pl.ds(row0 * SUB + j, rows, stride=SUB), :] for j in range(SUB)], axis=1)


def _start_row_gather(idx_ref, n_rows, src_hbm, dst, sem):
    def body(rb, carry):
        for u in range(SUB):
            tok = idx_ref[0, 0, rb * SUB + u]
            pltpu.make_async_copy(src_hbm.at[pl.ds(pl.multiple_of(tok * SUB, SUB), SUB)],
                                  dst.at[pl.ds(pl.multiple_of(rb * (SUB * SUB) + u * SUB, SUB), SUB)], sem).start()
        return carry
    lax.fori_loop(0, n_rows // SUB, body, 0)


def _wait_row_gather(n_rows, src_hbm, dst, sem):
    for r in range(n_rows):
        pltpu.make_async_copy(src_hbm.at[pl.ds(0, SUB)], dst.at[pl.ds(r * SUB, SUB)], sem).wait()


def _expert_kernel(te_ref, nv_ref, rt_cur, rt_nxt, hf_hbm, w1_ref, w3_ref, w2_ref, y_ref, xbuf, sem):
    s = pl.program_id(0)
    nv = nv_ref[0]
    slot = s % 2
    m = xbuf.shape[1] // SUB

    @pl.when(jnp.logical_and(s == 0, nv > 0))
    def _():
        _start_row_gather(rt_cur, m, hf_hbm, xbuf.at[0], sem.at[0])

    @pl.when(s + 1 < nv)
    def _():
        _start_row_gather(rt_nxt, m, hf_hbm, xbuf.at[1 - slot], sem.at[1 - slot])

    @pl.when(s < nv)
    def _():
        _wait_row_gather(m, hf_hbm, xbuf.at[slot], sem.at[slot])
        x = _from_token_tiles(xbuf.at[slot], 0, m).astype(BF16)
        h1 = jnp.dot(x, w1_ref[0], preferred_element_type=F32)
        h3 = jnp.dot(x, w3_ref[0], preferred_element_type=F32)
        a = (h1 * _sigmoid(h1) * h3).astype(BF16)
        _to_token_tiles(y_ref, jnp.dot(a, w2_ref[0], preferred_element_type=F32))

    @pl.when(s >= nv)
    def _():
        y_ref[...] = jnp.zeros_like(y_ref)


def _expert_call(tile_expert, n_valid, row_token, hf_tiles, w1, w3, w2):
    n_tiles = tile_expert.shape[0]
    d, de = w1.shape[1], w1.shape[2]
    assert d == SUB * LANES
    m = TM
    grid_spec = pltpu.PrefetchScalarGridSpec(
        num_scalar_prefetch=2,
        grid=(n_tiles,),
        in_specs=[pl.BlockSpec((1, 1, m), lambda s, te, nv: (s, 0, 0), memory_space=pltpu.SMEM),
                  pl.BlockSpec((1, 1, m), lambda s, te, nv: (jnp.minimum(s + 1, n_tiles - 1), 0, 0),
                               memory_space=pltpu.SMEM),
                  pl.BlockSpec(memory_space=pl.ANY),
                  pl.BlockSpec((1, d, de), lambda s, te, nv: (te[s], 0, 0)),
                  pl.BlockSpec((1, d, de), lambda s, te, nv: (te[s], 0, 0)),
                  pl.BlockSpec((1, de, d), lambda s, te, nv: (te[s], 0, 0))],
        out_specs=pl.BlockSpec((m * SUB, LANES), lambda s, te, nv: (s, 0)),
        scratch_shapes=[pltpu.VMEM((2, m * SUB, LANES), F32), pltpu.SemaphoreType.DMA((2,))],
    )
    return pl.pallas_call(
        _expert_kernel,
        grid_spec=grid_spec,
        out_shape=jax.ShapeDtypeStruct((n_tiles * m * SUB, LANES), F32),
        compiler_params=_cparams(1),
        name="moe_experts",
    )(tile_expert, n_valid, row_token, row_token, hf_tiles, w1, w3, w2)


def _combine_kernel(pos_cur, pos_nxt, x1_ref, mod_ref, ri_ref, gfin_ref, y_hbm, o_ref, ybuf, sem, *, final):
    bi, i = pl.program_id(0), pl.program_id(1)
    nb, ni = pl.num_programs(0), pl.num_programs(1)
    step = bi * ni + i
    slot = step % 2
    rows = ybuf.shape[1] // SUB

    @pl.when(step == 0)
    def _():
        _start_row_gather(pos_cur, rows, y_hbm, ybuf.at[0], sem.at[0])

    @pl.when(step + 1 < nb * ni)
    def _():
        _start_row_gather(pos_nxt, rows, y_hbm, ybuf.at[1 - slot], sem.at[1 - slot])

    _wait_row_gather(rows, y_hbm, ybuf.at[slot], sem.at[slot])
    g_f = mod_ref[0, 0][5:6]
    ri = ri_ref[0]
    p1, p2 = ri[:, 2:3], ri[:, 3:4]
    half = rows // 2
    y = p1 * _from_token_tiles(ybuf.at[slot], 0, half) + p2 * _from_token_tiles(ybuf.at[slot], half, half)
    x2 = x1_ref[0] + g_f * y
    o_ref[0] = _rms(x2, gfin_ref[...]) if final else x2


def _combine_call(pos, x1, mod6, rinfo, gfin, ysorted, n_lat_tiles, final):
    b, n, d = x1.shape
    nt = n // TM
    ni = n_lat_tiles if final else nt

    def pos_tile(step):
        return (step // ni) * nt + step % ni

    def tok(width):
        return pl.BlockSpec((1, TM, width), lambda bi, i: (bi, i, 0))

    kern = functools.partial(_combine_kernel, final=final)
    return pl.pallas_call(
        kern,
        grid=(b, ni),
        in_specs=[pl.BlockSpec((1, 1, 2 * TM), lambda bi, i: (pos_tile(bi * ni + i), 0, 0), memory_space=pltpu.SMEM),
                  pl.BlockSpec((1, 1, 2 * TM),
                               lambda bi, i: (pos_tile(jnp.minimum(bi * ni + i + 1, b * ni - 1)), 0, 0),
                               memory_space=pltpu.SMEM),
                  tok(d),
                  pl.BlockSpec((1, 1, 6, d), lambda bi, i: (bi, jnp.where(i < n_lat_tiles, 1, 0), 0, 0)),
                  tok(LANES),
                  pl.BlockSpec((1, d), lambda bi, i: (0, 0)),
                  pl.BlockSpec(memory_space=pl.ANY)],
        out_specs=tok(d),
        out_shape=jax.ShapeDtypeStruct((b, ni * TM, d), F32),
        scratch_shapes=[pltpu.VMEM((2, 2 * TM * SUB, LANES), F32), pltpu.SemaphoreType.DMA((2,))],
        compiler_params=_cparams(2),
        name="moe_combine_final" if final else "moe_combine",
    )(pos, pos, x1, mod6, rinfo, gfin, ysorted)


def _route(rinfo, b, n):
    t = b * n
    m = TM
    e = rinfo[..., :2].astype(jnp.int32).reshape(t, 2)
    ef = e.T.reshape(-1)
    oh = (ef[:, None] == jnp.arange(N_EXPERTS, dtype=jnp.int32)[None, :]).astype(jnp.int32)
    csum = jnp.cumsum(oh, axis=0)
    rank = jnp.take_along_axis(csum, ef[:, None], axis=1)[:, 0] - 1
    counts = csum[-1]
    padded = ((counts + m - 1) // m) * m
    ends = jnp.cumsum(padded)
    pos = (ends - padded)[ef] + rank
    n_tiles = (2 * t) // m + N_EXPERTS
    n_valid = (ends[-1] // m).astype(jnp.int32).reshape(1)
    tile_start = jnp.arange(n_tiles, dtype=jnp.int32) * m
    tile_expert = jnp.minimum(jnp.sum((tile_start[:, None] >= ends[None, :]).astype(jnp.int32), axis=1),
                              N_EXPERTS - 1).astype(jnp.int32)
    row_token = jnp.zeros((n_tiles * m,), jnp.int32).at[pos].set(jnp.arange(2 * t, dtype=jnp.int32) % t)
    pos_tiles = pos.reshape(2, t // m, 1, m).transpose(1, 2, 0, 3).reshape(t // m, 1, 2 * m)
    return tile_expert, n_valid, row_token.reshape(n_tiles, 1, m), pos_tiles


def _prep_w_in(w_in):
    d = w_in.shape[0]
    o = np.cumsum([0, LRU_WIDTH, MLA_KV_RANK, MLA_ROPE, GQA_KV_HEADS * GQA_DIM, GQA_KV_HEADS * GQA_DIM,
                   LRU_WIDTH, MLA_Q_RANK, GQA_HEADS * GQA_DIM, N_BRANCH * d])
    xr, ckv, kr, gk, gv, rg, cq, gq, mg = [w_in[:, int(o[i]):int(o[i + 1])] for i in range(9)]
    z = lambda w: jnp.zeros((d, w), w_in.dtype)
    return jnp.concatenate([xr, rg, mg, gq, cq, ckv, z(ROPE_LANE0), kr, z(LANES - ROPE_LANE0 - MLA_ROPE), gk, gv],
                           axis=1).astype(BF16)


def _prep_wuq(wuq):
    r = wuq.shape[0]
    w = wuq.reshape(r, MLA_HEADS, MLA_NOPE + MLA_ROPE)
    w = jnp.pad(w, ((0, 0), (0, 0), (0, MLA_SLOT - MLA_NOPE - MLA_ROPE)))
    return w.reshape(r, MLA_HEADS * MLA_SLOT).astype(BF16)


def _prep_wukv(wukv):
    r = wukv.shape[0]
    w = wukv.reshape(r, MLA_HEADS, MLA_NOPE + MLA_V)
    k = jnp.pad(w[:, :, :MLA_NOPE], ((0, 0), (0, 0), (0, MLA_SLOT - MLA_NOPE))).reshape(r, MLA_HEADS * MLA_SLOT)
    v = w[:, :, MLA_NOPE:].reshape(r, MLA_HEADS * MLA_V)
    return jnp.concatenate([k, v], axis=1).astype(BF16)


def _place_matrix():
    p = np.zeros((LANES, MLA_HEADS * MLA_SLOT), np.float32)
    for hh in range(MLA_HEADS):
        for r in range(MLA_ROPE):
            p[ROPE_LANE0 + r, hh * MLA_SLOT + ROPE_LANE0 + r] = 1.0
    return jnp.asarray(p, BF16)


def _kv_select_matrices():
    kvw = GQA_KV_HEADS * GQA_DIM
    sk = np.zeros((GQA_KV_HEADS, kvw, LANES), np.float32)
    svt = np.zeros((GQA_KV_HEADS, VT_ROWS, kvw), np.float32)
    for g in range(GQA_KV_HEADS):
        for c in range(GQA_DIM):
            sk[g, g * GQA_DIM + c, c] = 1.0
            svt[g, c, g * GQA_DIM + c] = 1.0
    return jnp.asarray(sk, BF16), jnp.asarray(svt, BF16)


def _block_diag_pairs(w):
    per = LANES // LRU_BLOCK
    nd = w.shape[0]
    w = w.reshape(nd, LRU_BLOCKS // per, per, LRU_BLOCK, LRU_BLOCK)
    eye = jnp.eye(per, dtype=w.dtype)
    bd = jnp.einsum("dgpij,pq->dgpiqj", w, eye)
    return bd.reshape(nd, LRU_BLOCKS // per, LANES, LANES)


def kernel(x, c, ctx, c_ctx, w_mod, b_mod, g_mix, g_ffn, w_in, conv_w, conv_b, lru_wa, lru_ba, lru_wi, lru_bi,
           lru_lambda, mla_gq, mla_wuq, mla_gkv, mla_wukv, gqa_gq, gqa_gk, w_branch, w_out, moe_wg, moe_bg,
           moe_we, moe_be, moe_w1, moe_w3, moe_w2, g_final):
    b, seq, d = x.shape
    n_ctx = ctx.shape[1]
    n = n_ctx + seq
    depth = w_mod.shape[0]
    assert n_ctx == TM and seq % KC == 0 and seq % GRID_W == 0 and d == 1024
    n_lat_tiles = seq // TM

    xx = jnp.concatenate([x, ctx], axis=1)
    tab_m, tab_g = _rope_tables(n_ctx, seq)
    place = _place_matrix()
    selk, selv = _kv_select_matrices()
    mod_rows = 16
    cc = jnp.concatenate([c, c_ctx[None, :], jnp.zeros((mod_rows - b - 1, d), F32)], axis=0)

    out = None
    for l in range(depth):
        mod = _mod_call(cc, w_mod[l], b_mod[l])
        mod6 = jnp.stack([jnp.broadcast_to(mod[b].reshape(1, 6, d), (b, 6, d)), mod[:b].reshape(b, 6, d)], axis=1)

        xr, rg, mg, qm, km, vm, qg, kg, vg = _inproj_call(
            xx, mod6, g_mix[l].reshape(1, d), _prep_w_in(w_in[l]), _prep_wuq(mla_wuq[l]), _prep_wukv(mla_wukv[l]),
            place, mla_gq[l].reshape(1, -1), mla_gkv[l].reshape(1, -1),
            jnp.tile(gqa_gq[l], LANES // GQA_DIM).reshape(1, LANES),
            jnp.tile(gqa_gk[l], LANES // GQA_DIM).reshape(1, LANES), tab_m, tab_g, n_lat_tiles)

        xr_t = jnp.transpose(xr, (1, 0, 2))
        wa_bd, wi_bd = _block_diag_pairs(lru_wa[l]), _block_diag_pairs(lru_wi[l])
        lru_args = lambda dr: (conv_w[l], conv_b[l].reshape(1, -1), wa_bd[dr], wi_bd[dr],
                               lru_ba[l, dr].reshape(1, -1), lru_bi[l, dr].reshape(1, -1),
                               lru_lambda[l, dr].reshape(1, -1))
        y_f = _lru_call(xr_t, None, *lru_args(0), n_ctx, False)
        y_t = _lru_call(xr_t, y_f, *lru_args(1), n_ctx, True)
        ylru = jnp.transpose(y_t, (1, 0, 2))

        ym_lat = _mla_attn_call(qm, km, vm, seq, n_ctx, False)
        ym_ctx = _mla_attn_call(qm, km, vm, seq, n_ctx, True)
        yg_lat = _gqa_attn_call(qg, kg, vg, selk, selv, seq, n_ctx, False)
        yg_ctx = _gqa_attn_call(qg, kg, vg, selk, selv, seq, n_ctx, True)

        wr = jnp.concatenate([moe_wg[l], moe_we[l], jnp.zeros((d, LANES - N_GROUPS - N_EXPERTS), F32)], axis=1)
        wr_hi = wr.astype(BF16)
        wr = jnp.stack([wr_hi, (wr - wr_hi.astype(F32)).astype(BF16)])
        br = jnp.concatenate([moe_bg[l], moe_be[l], jnp.zeros((LANES - N_GROUPS - N_EXPERTS,), F32)]).reshape(1, LANES)
        x1, hf, rinfo = _merge_call(xx, mod6, ylru, rg, ym_lat, ym_ctx, yg_lat, yg_ctx, mg,
                                    w_branch[l].astype(BF16), w_out[l].astype(BF16),
                                    g_ffn[l].reshape(1, d), wr, br, n_lat_tiles)

        tile_expert, n_valid, row_token, pos_tiles = _route(rinfo, b, n)
        ysorted = _expert_call(tile_expert, n_valid, row_token, hf,
                               moe_w1[l].astype(BF16), moe_w3[l].astype(BF16), moe_w2[l].astype(BF16))
        final = l == depth - 1
        out = _combine_call(pos_tiles, x1, mod6, rinfo, g_final.reshape(1, d), ysorted, n_lat_tiles, final)
        xx = out
    return out
```

```python
import functools

import numpy as np
import jax
import jax.numpy as jnp
from jax import lax
from jax.experimental import pallas as pl
from jax.experimental.pallas import tpu as pltpu

F32 = jnp.float32
BF16 = jnp.bfloat16

EPS = 1e-6
ROPE_THETA = 10000.0
GRID_W = 64
LOG2E = 1.4426950408889634

LRU_WIDTH = 512
LRU_BLOCKS = 8
LRU_BLOCK = LRU_WIDTH // LRU_BLOCKS
CONV_W = 4
LRU_C = 8.0
MLA_HEADS = 8
MLA_Q_RANK = 256
MLA_KV_RANK = 128
MLA_NOPE = 64
MLA_ROPE = 32
MLA_V = 64
MLA_SCALE = (MLA_NOPE + MLA_ROPE) ** -0.5
GQA_HEADS = 8
GQA_KV_HEADS = 2
GQA_GROUP = GQA_HEADS // GQA_KV_HEADS
GQA_DIM = 64
GQA_SCALE = GQA_DIM ** -0.5
N_BRANCH = 3
BRANCH_W = 512
N_GROUPS = 4
EXPERTS_PER_GROUP = 8
N_EXPERTS = N_GROUPS * EXPERTS_PER_GROUP
D_EXPERT = 256

LANES = 128
TM = 256
KC = 1024
MLA_RQ = 512
MLA_HP = 4
GQA_RQ = 512
VMEM_LIMIT = 56 * 1024 * 1024

C_XR = 0
C_RG = C_XR + LRU_WIDTH
C_MG = C_RG + LRU_WIDTH
C_GQ = C_MG + N_BRANCH * 1024
C_SMALL = C_GQ + GQA_HEADS * GQA_DIM
S_CQ = 0
S_CKV = S_CQ + MLA_Q_RANK
S_KR = S_CKV + MLA_KV_RANK
S_GK = S_KR + LANES
S_GV = S_GK + GQA_KV_HEADS * GQA_DIM
SMALL_W = S_GV + GQA_KV_HEADS * GQA_DIM
MLA_SLOT = 128
ROPE_LANE0 = MLA_NOPE
ONES_ROW = 64
VT_ROWS = 80
assert MLA_V == ONES_ROW and GQA_DIM == ONES_ROW


def _cparams(n_axes):
    return pltpu.CompilerParams(dimension_semantics=("arbitrary",) * n_axes,
                                vmem_limit_bytes=VMEM_LIMIT)


def _rms(x, g):
    return x * lax.rsqrt(jnp.mean(x * x, axis=-1, keepdims=True) + EPS) * g


def _sigmoid(x):
    return 1.0 / (1.0 + jnp.exp(-x))


def _lane_iota(shape):
    return lax.broadcasted_iota(jnp.int32, shape, len(shape) - 1)


def _mod_kernel(c_ref, w_ref, b_ref, o_ref):
    c = c_ref[...]
    s = c * _sigmoid(c)
    o_ref[...] = jnp.dot(s, w_ref[...], preferred_element_type=F32) + b_ref[...]


def _mod_call(cc, w, b):
    rows, d = cc.shape
    n = w.shape[1]
    bn = 512
    return pl.pallas_call(
        _mod_kernel,
        grid=(n // bn,),
        in_specs=[pl.BlockSpec((rows, d), lambda j: (0, 0)),
                  pl.BlockSpec((d, bn), lambda j: (0, j)),
                  pl.BlockSpec((1, bn), lambda j: (0, j))],
        out_specs=pl.BlockSpec((rows, bn), lambda j: (0, j)),
        out_shape=jax.ShapeDtypeStruct((rows, n), F32),
        compiler_params=_cparams(1),
        name="adaln_mod",
    )(cc, w, b.reshape(1, n))


def _rope_tables(n_ctx, seq):
    rows = seq // GRID_W
    row = np.repeat(np.arange(rows), GRID_W).astype(np.float64)
    col = np.tile(np.arange(GRID_W), rows).astype(np.float64)

    def angles(rot_dim):
        quarter = rot_dim // 4
        freqs = ROPE_THETA ** (-np.arange(quarter, dtype=np.float64) / quarter)
        return np.concatenate([row[:, None] * freqs, col[:, None] * freqs], axis=-1)

    n = n_ctx + seq
    am = angles(MLA_ROPE)
    half = MLA_ROPE // 2
    cm = np.ones((n, LANES)); sm = np.zeros((n, LANES))
    cm[:seq, ROPE_LANE0:ROPE_LANE0 + half] = np.cos(am)
    cm[:seq, ROPE_LANE0 + half:ROPE_LANE0 + 2 * half] = np.cos(am)
    sm[:seq, ROPE_LANE0:ROPE_LANE0 + half] = -np.sin(am)
    sm[:seq, ROPE_LANE0 + half:ROPE_LANE0 + 2 * half] = np.sin(am)
    ag = angles(GQA_DIM)
    hg = GQA_DIM // 2
    cg = np.ones((n, LANES)); sg = np.zeros((n, LANES))
    for h0 in (0, GQA_DIM):
        cg[:seq, h0:h0 + hg] = np.cos(ag)
        cg[:seq, h0 + hg:h0 + 2 * hg] = np.cos(ag)
        sg[:seq, h0:h0 + hg] = -np.sin(ag)
        sg[:seq, h0 + hg:h0 + 2 * hg] = np.sin(ag)
    tab_m = np.concatenate([cm, sm], axis=-1).astype(np.float32)
    tab_g = np.concatenate([cg, sg], axis=-1).astype(np.float32)
    return jnp.asarray(tab_m), jnp.asarray(tab_g)


def _rope_mla(y, cos, sin):
    half = MLA_ROPE // 2
    lane = _lane_iota(y.shape)
    rot = jnp.where(lane < ROPE_LANE0 + half,
                    pltpu.roll(y, LANES - half, 1),
                    pltpu.roll(y, half, 1))
    return y * cos + rot * sin


def _rope_gqa(y, cos, sin):
    half = GQA_DIM // 2
    lane = _lane_iota(y.shape)
    rot = jnp.where((lane % GQA_DIM) < half,
                    pltpu.roll(y, LANES - half, 1),
                    pltpu.roll(y, half, 1))
    return y * cos + rot * sin


def _head_rms_pair(t, g):
    lane = _lane_iota(t.shape)
    lo = lane < GQA_DIM
    t2 = t * t
    s_lo = jnp.sum(jnp.where(lo, t2, 0.0), axis=-1, keepdims=True)
    s_hi = jnp.sum(jnp.where(lo, 0.0, t2), axis=-1, keepdims=True)
    r = jnp.where(lo, lax.rsqrt(s_lo * (1.0 / GQA_DIM) + EPS), lax.rsqrt(s_hi * (1.0 / GQA_DIM) + EPS))
    return t * r * g


def _inproj_kernel(x_ref, mod_ref, gmix_ref, w_ref, wuq_ref, wukv_ref, place_ref,
                   gq_ref, gkv_ref, ggq_ref, ggk_ref, tabm_ref, tabg_ref,
                   xr_ref, rg_ref, mg_ref, qm_ref, km_ref, vm_ref, qg_ref, kg_ref, vg_ref):
    x = x_ref[0]
    m6 = mod_ref[0, 0]
    sh, sc = m6[0:1], m6[1:2]
    h = (_rms(x, gmix_ref[...]) * (1.0 + sc) + sh).astype(BF16)

    def proj(c0, width):
        return jnp.dot(h, w_ref[:, c0:c0 + width], preferred_element_type=F32)

    cos_m, sin_m = tabm_ref[:, :LANES], tabm_ref[:, LANES:]
    cos_g, sin_g = tabg_ref[:, :LANES], tabg_ref[:, LANES:]
    lane = _lane_iota((TM, LANES))

    small = proj(C_SMALL, SMALL_W)
    gq = proj(C_GQ, GQA_HEADS * GQA_DIM)
    cq = small[:, S_CQ:S_CQ + MLA_Q_RANK]
    ckv = small[:, S_CKV:S_CKV + MLA_KV_RANK]
    krp = small[:, S_KR:S_KR + LANES]
    gk = small[:, S_GK:S_GK + LANES]
    gv = small[:, S_GV:S_GV + LANES]
    xr_ref[0] = proj(C_XR, LRU_WIDTH)
    rg_ref[0] = proj(C_RG, LRU_WIDTH).astype(BF16)

    cqn = _rms(cq, gq_ref[...]).astype(BF16)
    ckvn = _rms(ckv, gkv_ref[...]).astype(BF16)
    kr = _rope_mla(krp, cos_m, sin_m).astype(BF16)
    qu = jnp.dot(cqn, wuq_ref[...], preferred_element_type=F32)
    kvu = jnp.dot(ckvn, wukv_ref[...], preferred_element_type=F32)
    kr_placed = jnp.dot(kr, place_ref[...], preferred_element_type=F32)

    for j in range(N_BRANCH * 1024 // 512):
        mg_ref[0, :, j * 512:(j + 1) * 512] = proj(C_MG + j * 512, 512).astype(BF16)

    for v in range(GQA_HEADS * GQA_DIM // LANES):
        t = gq[:, v * LANES:(v + 1) * LANES]
        y = _rope_gqa(_head_rms_pair(t, ggq_ref[...]), cos_g, sin_g) * (GQA_SCALE * LOG2E)
        ysw = pltpu.roll(y, GQA_DIM, 1)
        qg_ref[0, :, (2 * v) * LANES:(2 * v + 1) * LANES] = jnp.where(lane < GQA_DIM, y, ysw).astype(BF16)
        qg_ref[0, :, (2 * v + 1) * LANES:(2 * v + 2) * LANES] = jnp.where(lane < GQA_DIM, ysw, y).astype(BF16)
    kg_ref[0] = _rope_gqa(_head_rms_pair(gk, ggk_ref[...]), cos_g, sin_g).astype(BF16)
    vg_ref[0] = gv.astype(BF16)

    for hh in range(MLA_HEADS):
        y = _rope_mla(qu[:, hh * MLA_SLOT:(hh + 1) * MLA_SLOT], cos_m, sin_m)
        qm_ref[0, :, hh * MLA_SLOT:(hh + 1) * MLA_SLOT] = (y * (MLA_SCALE * LOG2E)).astype(BF16)
    n_k = MLA_HEADS * MLA_SLOT
    km_ref[0] = (kvu[:, :n_k] + kr_placed).astype(BF16)
    vm_ref[0] = kvu[:, n_k:].astype(BF16)


def _inproj_call(xx, mod6, gmix, w_all, wuq, wukv, place, gq, gkv, ggq, ggk, tab_m, tab_g, n_lat_tiles):
    b, n, d = xx.shape
    nt = n // TM
    ncol = w_all.shape[1]

    def tok(width):
        return pl.BlockSpec((1, TM, width), lambda bi, i: (bi, i, 0))

    def const(shape):
        return pl.BlockSpec(shape, lambda bi, i: (0,) * len(shape))

    out_w = [(LRU_WIDTH, F32), (LRU_WIDTH, BF16), (N_BRANCH * 1024, BF16),
             (MLA_HEADS * MLA_SLOT, BF16), (MLA_HEADS * MLA_SLOT, BF16), (MLA_HEADS * MLA_V, BF16),
             (GQA_HEADS * LANES, BF16), (GQA_KV_HEADS * GQA_DIM, BF16), (GQA_KV_HEADS * GQA_DIM, BF16)]
    return pl.pallas_call(
        _inproj_kernel,
        grid=(b, nt),
        in_specs=[tok(d),
                  pl.BlockSpec((1, 1, 6, d), lambda bi, i: (bi, jnp.where(i < n_lat_tiles, 1, 0), 0, 0)),
                  const((1, d)), const((d, ncol)), const(wuq.shape), const(wukv.shape), const(place.shape),
                  const((1, MLA_Q_RANK)), const((1, MLA_KV_RANK)), const((1, LANES)), const((1, LANES)),
                  pl.BlockSpec((TM, 2 * LANES), lambda bi, i: (i, 0)),
                  pl.BlockSpec((TM, 2 * LANES), lambda bi, i: (i, 0))],
        out_specs=[tok(w) for w, _ in out_w],
        out_shape=[jax.ShapeDtypeStruct((b, n, w), dt) for w, dt in out_w],
        compiler_params=_cparams(2),
        name="in_proj",
    )(xx, mod6, gmix, w_all, wuq, wukv, place, gq, gkv, ggq, ggk, tab_m, tab_g)


def _flash_t(streams, n_main, tail_start, tail_size):
    nt_dims = (((1,), (1,)), ((), ()))
    ppc = KC // TM

    def scores(start, size):
        return tuple(lax.dot_general(k_fn(start, size), q, nt_dims, preferred_element_type=F32)
                     for q, k_fn, _ in streams)

    def softmax_pv(sts, size, piece0, carries):
        stats = []
        for st, (m, _) in zip(sts, carries):
            m_new = jnp.maximum(m, jnp.max(st, axis=0, keepdims=True))
            stats.append((m_new, jnp.exp2(m - m_new), jnp.exp2(st - m_new).astype(BF16)))
        out = []
        for (_, _, vt_fn), (m_new, alpha, pb), (_, acc) in zip(streams, stats, carries):
            pv = None
            for t in range(size // TM):
                part = jnp.dot(vt_fn(piece0 + t), pb[t * TM:(t + 1) * TM], preferred_element_type=F32)
                pv = part if pv is None else pv + part
            out.append((m_new, alpha * acc + pv))
        return tuple(out)

    carries = tuple((jnp.full((1, q.shape[0]), -jnp.inf, F32), jnp.zeros((VT_ROWS, q.shape[0]), F32))
                    for q, _, _ in streams)
    chunks = [(j * KC, KC) for j in range(n_main)] + [(tail_start, tail_size)]
    sts = scores(*chunks[0])
    for idx, (start, size) in enumerate(chunks):
        nxt = scores(*chunks[idx + 1]) if idx + 1 < len(chunks) else None
        carries = softmax_pv(sts, size, start // TM, carries)
        sts = nxt
    return [acc[:ONES_ROW] * (1.0 / acc[ONES_ROW:ONES_ROW + 1]) for _, acc in carries]


def _values_t(sel, v, row):
    vt = lax.dot_general(sel, v, (((1,), (1,)), ((), ())), preferred_element_type=F32)
    return jnp.where(row == ONES_ROW, 1.0, vt).astype(BF16)


def _mla_attn_kernel(q_ref, k_ref, v_ref, o_ref, vt, *, hp, n_main, tail_start, tail_size):
    nk = k_ref.shape[1]

    @pl.when(pl.program_id(2) == 0)
    def _():
        row = lax.broadcasted_iota(jnp.int32, (VT_ROWS, TM), 0)
        r_i = lax.broadcasted_iota(jnp.int32, (VT_ROWS, LANES), 0)
        c_i = _lane_iota((VT_ROWS, LANES))
        for hh in range(hp):
            sel = jnp.where(jnp.logical_and(r_i < MLA_V, c_i == r_i + (hh % 2) * MLA_V), 1.0, 0.0).astype(BF16)
            for c in range(nk // TM):
                v = v_ref[0, c * TM:(c + 1) * TM, (hh // 2) * LANES:(hh // 2 + 1) * LANES]
                vt[hh, c] = _values_t(sel, v, row)

    streams = []
    for hh in range(hp):
        q = q_ref[0, :, hh * MLA_SLOT:(hh + 1) * MLA_SLOT]
        k_fn = functools.partial(lambda st, sz, c0: k_ref[0, pl.ds(st, sz), c0:c0 + MLA_SLOT], c0=hh * MLA_SLOT)
        vt_fn = functools.partial(lambda piece, h: vt[h, piece], h=hh)
        streams.append((q, k_fn, vt_fn))
    outs = _flash_t(streams, n_main, tail_start, tail_size)
    for pr in range(hp // 2):
        o_t = jnp.concatenate([outs[2 * pr], outs[2 * pr + 1]], axis=0)
        o_ref[0, :, pr * LANES:(pr + 1) * LANES] = o_t.T.astype(BF16)


def _mla_attn_call(qm, km, vm, seq, n_ctx, ctx_queries):
    b = qm.shape[0]
    hp = MLA_HP
    if ctx_queries:
        rq, nq, q0, nk, k0 = n_ctx, 1, seq // n_ctx, n_ctx, seq // n_ctx
        plan = dict(n_main=0, tail_start=0, tail_size=n_ctx)
    else:
        rq, nq, q0, nk, k0 = MLA_RQ, seq // MLA_RQ, 0, seq + n_ctx, 0
        plan = dict(n_main=seq // KC - 1, tail_start=seq - KC, tail_size=KC + n_ctx)
    kern = functools.partial(_mla_attn_kernel, hp=hp, **plan)
    return pl.pallas_call(
        kern,
        grid=(b, MLA_HEADS // hp, nq),
        in_specs=[pl.BlockSpec((1, rq, hp * MLA_SLOT), lambda bi, p, i: (bi, i + q0, p)),
                  pl.BlockSpec((1, nk, hp * MLA_SLOT), lambda bi, p, i: (bi, k0, p)),
                  pl.BlockSpec((1, nk, hp * MLA_V), lambda bi, p, i: (bi, k0, p))],
        out_specs=pl.BlockSpec((1, rq, hp * MLA_V), lambda bi, p, i: (bi, i, p)),
        out_shape=jax.ShapeDtypeStruct((b, rq * nq, MLA_HEADS * MLA_V), BF16),
        scratch_shapes=[pltpu.VMEM((hp, nk // TM, VT_ROWS, TM), BF16)],
        compiler_params=_cparams(3),
        name="mla_attention_ctx" if ctx_queries else "mla_attention",
    )(qm, km, vm)


def _gqa_attn_kernel(q_ref, k_ref, v_ref, selk_ref, selvt_ref, o_ref, kd, vt, *, n_main, tail_start, tail_size):
    i = pl.program_id(2)
    nk = kd.shape[0]
    rq = q_ref.shape[1]

    @pl.when(i == 0)
    def _():
        row = lax.broadcasted_iota(jnp.int32, (VT_ROWS, TM), 0)
        for c in range(nk // TM):
            rows = pl.ds(c * TM, TM)
            kd[rows, :] = jnp.dot(k_ref[0, rows, :], selk_ref[0], preferred_element_type=F32).astype(BF16)
            vt[c] = _values_t(selvt_ref[0], v_ref[0, rows, :], row)

    streams = [(q_ref[0, :, j * LANES:(j + 1) * LANES], lambda st, sz: kd[pl.ds(st, sz), :], lambda piece: vt[piece])
               for j in range(GQA_GROUP)]
    outs = _flash_t(streams, n_main, tail_start, tail_size)
    for pr in range(GQA_GROUP // 2):
        o_t = jnp.concatenate([outs[2 * pr], outs[2 * pr + 1]], axis=0)
        o_ref[0, :, pr * LANES:(pr + 1) * LANES] = o_t.T.astype(BF16)


def _gqa_attn_call(qg, kg, vg, selk, selv, seq, n_ctx, ctx_queries):
    b = qg.shape[0]
    gw = GQA_GROUP * GQA_DIM
    kvw = GQA_KV_HEADS * GQA_DIM
    if ctx_queries:
        rq, nq, q0, nk, k0 = n_ctx, 1, seq // n_ctx, n_ctx, seq // n_ctx
        plan = dict(n_main=0, tail_start=0, tail_size=n_ctx)
    else:
        rq, nq, q0, nk, k0 = GQA_RQ, seq // GQA_RQ, 0, seq + n_ctx, 0
        plan = dict(n_main=seq // KC - 1, tail_start=seq - KC, tail_size=KC + n_ctx)
    kern = functools.partial(_gqa_attn_kernel, **plan)
    return pl.pallas_call(
        kern,
        grid=(b, GQA_KV_HEADS, nq),
        in_specs=[pl.BlockSpec((1, rq, GQA_GROUP * LANES), lambda bi, g, i: (bi, i + q0, g)),
                  pl.BlockSpec((1, nk, kvw), lambda bi, g, i: (bi, k0, 0)),
                  pl.BlockSpec((1, nk, kvw), lambda bi, g, i: (bi, k0, 0)),
                  pl.BlockSpec((1, kvw, LANES), lambda bi, g, i: (g, 0, 0)),
                  pl.BlockSpec((1, VT_ROWS, kvw), lambda bi, g, i: (g, 0, 0))],
        out_specs=pl.BlockSpec((1, rq, gw), lambda bi, g, i: (bi, i, g)),
        out_shape=jax.ShapeDtypeStruct((b, rq * nq, GQA_HEADS * GQA_DIM), BF16),
        scratch_shapes=[pltpu.VMEM((nk, LANES), BF16), pltpu.VMEM((nk // TM, VT_ROWS, TM), BF16)],
        compiler_params=_cparams(3),
        name="gqa_attention_ctx" if ctx_queries else "gqa_attention",
    )(qg, kg, vg, selk, selv)


def _softplus(z):
    return jnp.maximum(z, 0.0) + jnp.log1p(jnp.exp(-jnp.abs(z)))


def _lru_chunk(j, reverse, n_lat_chunks):
    return jnp.where(j == 0, n_lat_chunks, n_lat_chunks - j if reverse else j - 1)


def _lru_kernel(*refs, reverse, n_chunks, n_lat_chunks):
    if reverse:
        (x_ref, hp_ref, hn_ref, cw_ref, cb_ref, wa_ref, wi_ref, ba_ref, bi_ref, lam_ref, yin_ref,
         y_ref, xs, a_s, b_s, h_s) = refs
    else:
        (x_ref, hp_ref, hn_ref, cw_ref, cb_ref, wa_ref, wi_ref, ba_ref, bi_ref, lam_ref,
         y_ref, xs, a_s, b_s, h_s) = refs
        yin_ref = None
    j = pl.program_id(1)
    c = _lru_chunk(j, reverse, n_lat_chunks)
    ch, bsz, lw = x_ref.shape

    @pl.when(j == 0)
    def _():
        h_s[...] = jnp.zeros_like(h_s)

    seq_first = jnp.logical_or(c == 0, c == n_lat_chunks)
    seq_last = jnp.logical_or(c == n_lat_chunks - 1, c == n_chunks - 1)
    xs[0:2] = jnp.where(seq_first, 0.0, hp_ref[...])
    xs[2:2 + ch] = x_ref[...]
    xs[2 + ch:3 + ch] = jnp.where(seq_last, 0.0, hn_ref[...])
    xc = cb_ref[...]
    for tap in range(CONV_W):
        xc = xc + cw_ref[tap:tap + 1, :] * xs[tap:tap + ch]
    x2 = xc.reshape(ch * bsz, lw)
    r = _sigmoid(jnp.dot(x2, wa_ref[0], preferred_element_type=F32) + ba_ref[...])
    gi = _sigmoid(jnp.dot(x2, wi_ref[0], preferred_element_type=F32) + bi_ref[...])
    log_a = (-LRU_C) * r * _softplus(-lam_ref[...])
    a = jnp.exp(log_a)
    a_s[...] = a.reshape(ch, bsz, lw)
    mult = jnp.sqrt(-jnp.tanh(log_a) * (a * a + 1.0))
    b_s[...] = (mult * gi * x2).reshape(ch, bsz, lw)

    def body(tt, h):
        t = ch - 1 - tt if reverse else tt
        h = a_s[t] * h + b_s[t]
        y_ref[t] = h + yin_ref[t] if reverse else h
        return h

    h_s[...] = lax.fori_loop(0, ch, body, h_s[...], unroll=8)


def _lru_call(xr_t, yin, cw, cb, wa_bd, wi_bd, ba, bi, lam, n_ctx, reverse):
    n, bsz, width = xr_t.shape
    ch = TM
    n_chunks = n // ch
    n_lat_chunks = (n - n_ctx) // ch
    groups = width // LANES
    cmap = functools.partial(_lru_chunk, reverse=reverse, n_lat_chunks=n_lat_chunks)

    main = pl.BlockSpec((ch, bsz, LANES), lambda g, j: (cmap(j), 0, g))
    in_specs = [main,
                pl.BlockSpec((2, bsz, LANES), lambda g, j: (jnp.maximum(cmap(j) * (ch // 2) - 1, 0), 0, g)),
                pl.BlockSpec((1, bsz, LANES), lambda g, j: (jnp.minimum((cmap(j) + 1) * ch, n - 1), 0, g)),
                pl.BlockSpec((CONV_W, LANES), lambda g, j: (0, g)),
                pl.BlockSpec((1, LANES), lambda g, j: (0, g)),
                pl.BlockSpec((1, LANES, LANES), lambda g, j: (g, 0, 0)),
                pl.BlockSpec((1, LANES, LANES), lambda g, j: (g, 0, 0)),
                pl.BlockSpec((1, LANES), lambda g, j: (0, g)),
                pl.BlockSpec((1, LANES), lambda g, j: (0, g)),
                pl.BlockSpec((1, LANES), lambda g, j: (0, g))]
    args = [xr_t, xr_t, xr_t, cw, cb, wa_bd, wi_bd, ba, bi, lam]
    if reverse:
        in_specs.append(main)
        args.append(yin)
    kern = functools.partial(_lru_kernel, reverse=reverse, n_chunks=n_chunks, n_lat_chunks=n_lat_chunks)
    return pl.pallas_call(
        kern,
        grid=(groups, n_chunks),
        in_specs=in_specs,
        out_specs=main,
        out_shape=jax.ShapeDtypeStruct((n, bsz, width), F32),
        scratch_shapes=[pltpu.VMEM((ch + 3, bsz, LANES), F32), pltpu.VMEM((ch, bsz, LANES), F32),
                        pltpu.VMEM((ch, bsz, LANES), F32), pltpu.VMEM((bsz, LANES), F32)],
        compiler_params=_cparams(2),
        name="rglru_rev" if reverse else "rglru_fwd",
    )(*args)


def _gelu_tanh(x):
    return 0.5 * x * (1.0 + jnp.tanh(0.7978845608028654 * (x + 0.044715 * (x * x * x))))


def _merge_kernel(x_ref, mod_ref, ylru_ref, rg_ref, yml_ref, ymc_ref, ygl_ref, ygc_ref, mg_ref, wbr_ref, wout_ref,
                  gffn_ref, wr_ref, br_ref, x1_ref, hf_ref, ri_ref, *, n_lat_tiles):
    d = x_ref.shape[-1]
    is_lat = pl.program_id(1) < n_lat_tiles
    m6 = mod_ref[0, 0]
    g_a, sh_f, sc_f = m6[2:3], m6[3:4], m6[4:5]
    y_rnn = (ylru_ref[0] * _gelu_tanh(rg_ref[0].astype(F32))).astype(BF16)
    y_mla = jnp.where(is_lat, yml_ref[0], ymc_ref[0])
    y_gqa = jnp.where(is_lat, ygl_ref[0], ygc_ref[0])
    acc = jnp.zeros((TM, d), F32)
    for kbr, br in enumerate((y_rnn, y_mla, y_gqa)):
        pr = jnp.dot(br, wbr_ref[kbr], preferred_element_type=F32)
        acc = acc + _sigmoid(mg_ref[0, :, kbr * d:(kbr + 1) * d].astype(F32)) * pr
    out = jnp.dot(acc.astype(BF16), wout_ref[...], preferred_element_type=F32)
    x1 = x_ref[0] + g_a * out
    x1_ref[0] = x1
    hf = _rms(x1, gffn_ref[...]) * (1.0 + sc_f) + sh_f
    _to_token_tiles(hf_ref, hf)
    hf_hi = hf.astype(BF16)
    hf_lo = (hf - hf_hi.astype(F32)).astype(BF16)
    lg = (jnp.dot(hf_hi, wr_ref[0], preferred_element_type=F32)
          + jnp.dot(hf_lo, wr_ref[0], preferred_element_type=F32)
          + jnp.dot(hf_hi, wr_ref[1], preferred_element_type=F32)) + br_ref[...]
    ri_ref[0] = _route_top2(lg)


def _route_top2(lg):
    lane = _lane_iota(lg.shape)
    big = jnp.int32(1 << 20)
    neg = -jnp.inf
    is_g = lane < N_GROUPS
    gl = jnp.where(is_g, lg, neg)
    gmax = jnp.max(gl, axis=-1, keepdims=True)
    gsel = jnp.min(jnp.where(gl == gmax, lane, big), axis=-1, keepdims=True)
    pg = 1.0 / jnp.sum(jnp.where(is_g, jnp.exp(lg - gmax), 0.0), axis=-1, keepdims=True)
    e0 = N_GROUPS + gsel * EXPERTS_PER_GROUP
    el = jnp.where(jnp.logical_and(lane >= e0, lane < e0 + EXPERTS_PER_GROUP), lg, neg)
    v1 = jnp.max(el, axis=-1, keepdims=True)
    i1 = jnp.min(jnp.where(el == v1, lane, big), axis=-1, keepdims=True)
    el2 = jnp.where(lane == i1, neg, el)
    v2 = jnp.max(el2, axis=-1, keepdims=True)
    i2 = jnp.min(jnp.where(el2 == v2, lane, big), axis=-1, keepdims=True)
    t = jnp.exp(v2 - v1)
    p1 = pg / (1.0 + t)
    p2 = p1 * t
    return jnp.where(lane == 0, (i1 - N_GROUPS).astype(F32),
                     jnp.where(lane == 1, (i2 - N_GROUPS).astype(F32),
                               jnp.where(lane == 2, p1, jnp.where(lane == 3, p2, 0.0))))


def _merge_call(xx, mod6, ylru, rg, ym_lat, ym_ctx, yg_lat, yg_ctx, mg, wbr, wout, gffn, wr, br, n_lat_tiles):
    b, n, d = xx.shape
    nt = n // TM

    def tok(width):
        return pl.BlockSpec((1, TM, width), lambda bi, i: (bi, i, 0))

    def lat(width):
        return pl.BlockSpec((1, TM, width), lambda bi, i: (bi, jnp.minimum(i, n_lat_tiles - 1), 0))

    def ctx(width):
        return pl.BlockSpec((1, TM, width), lambda bi, i: (bi, jnp.maximum(i - n_lat_tiles, 0), 0))

    def const(shape):
        return pl.BlockSpec(shape, lambda bi, i: (0,) * len(shape))

    return pl.pallas_call(
        functools.partial(_merge_kernel, n_lat_tiles=n_lat_tiles),
        grid=(b, nt),
        in_specs=[tok(d),
                  pl.BlockSpec((1, 1, 6, d), lambda bi, i: (bi, jnp.where(i < n_lat_tiles, 1, 0), 0, 0)),
                  tok(BRANCH_W), tok(BRANCH_W), lat(BRANCH_W), ctx(BRANCH_W), lat(BRANCH_W), ctx(BRANCH_W),
                  tok(N_BRANCH * d),
                  const(wbr.shape), const(wout.shape), const((1, d)), const(wr.shape), const((1, LANES))],
        out_specs=[tok(d), pl.BlockSpec((TM * SUB, LANES), lambda bi, i: (bi * nt + i, 0)), tok(LANES)],
        out_shape=[jax.ShapeDtypeStruct((b, n, d), F32), jax.ShapeDtypeStruct((b * n * SUB, LANES), F32),
                   jax.ShapeDtypeStruct((b, n, LANES), F32)],
        compiler_params=_cparams(2),
        name="merge_router",
    )(xx, mod6, ylru, rg, ym_lat, ym_ctx, yg_lat, yg_ctx, mg, wbr, wout, gffn, wr, br)


SUB = 8


def _to_token_tiles(ref, x, row0=0):
    rows = x.shape[0]
    for j in range(SUB):
        ref[pl.ds(row0 * SUB + j, rows, stride=SUB), :] = x[:, j * LANES:(j + 1) * LANES]


def _from_token_tiles(ref, row0, rows):
    return jnp.concatenate([ref[pl.ds(row0 * SUB + j, rows, stride=SUB), :] for j in range(SUB)], axis=1)


def _start_row_gather(idx_ref, n_rows, src_hbm, dst, sem):
    def body(rb, carry):
        for u in range(SUB):
            tok = idx_ref[0, 0, rb * SUB + u]
            pltpu.make_async_copy(src_hbm.at[pl.ds(pl.multiple_of(tok * SUB, SUB), SUB)],
                                  dst.at[pl.ds(pl.multiple_of(rb * (SUB * SUB) + u * SUB, SUB), SUB)], sem).start()
        return carry
    lax.fori_loop(0, n_rows // SUB, body, 0)


def _wait_row_gather(n_rows, src_hbm, dst, sem):
    for r in range(n_rows):
        pltpu.make_async_copy(src_hbm.at[pl.ds(0, SUB)], dst.at[pl.ds(r * SUB, SUB)], sem).wait()


def _expert_kernel(te_ref, nv_ref, rt_cur, rt_nxt, hf_hbm, w1_ref, w3_ref, w2_ref, y_ref, xbuf, sem):
    s = pl.program_id(0)
    nv = nv_ref[0]
    slot = s % 2
    m = xbuf.shape[1] // SUB

    @pl.when(jnp.logical_and(s == 0, nv > 0))
    def _():
        _start_row_gather(rt_cur, m, hf_hbm, xbuf.at[0], sem.at[0])

    @pl.when(s + 1 < nv)
    def _():
        _start_row_gather(rt_nxt, m, hf_hbm, xbuf.at[1 - slot], sem.at[1 - slot])

    @pl.when(s < nv)
    def _():
        _wait_row_gather(m, hf_hbm, xbuf.at[slot], sem.at[slot])
        x = _from_token_tiles(xbuf.at[slot], 0, m).astype(BF16)
        h1 = jnp.dot(x, w1_ref[0], preferred_element_type=F32)
        h3 = jnp.dot(x, w3_ref[0], preferred_element_type=F32)
        a = (h1 * _sigmoid(h1) * h3).astype(BF16)
        _to_token_tiles(y_ref, jnp.dot(a, w2_ref[0], preferred_element_type=F32))

    @pl.when(s >= nv)
    def _():
        y_ref[...] = jnp.zeros_like(y_ref)


def _expert_call(tile_expert, n_valid, row_token, hf_tiles, w1, w3, w2):
    n_tiles = tile_expert.shape[0]
    d, de = w1.shape[1], w1.shape[2]
    assert d == SUB * LANES
    m = TM
    grid_spec = pltpu.PrefetchScalarGridSpec(
        num_scalar_prefetch=2,
        grid=(n_tiles,),
        in_specs=[pl.BlockSpec((1, 1, m), lambda s, te, nv: (s, 0, 0), memory_space=pltpu.SMEM),
                  pl.BlockSpec((1, 1, m), lambda s, te, nv: (jnp.minimum(s + 1, n_tiles - 1), 0, 0),
                               memory_space=pltpu.SMEM),
                  pl.BlockSpec(memory_space=pl.ANY),
                  pl.BlockSpec((1, d, de), lambda s, te, nv: (te[s], 0, 0)),
                  pl.BlockSpec((1, d, de), lambda s, te, nv: (te[s], 0, 0)),
                  pl.BlockSpec((1, de, d), lambda s, te, nv: (te[s], 0, 0))],
        out_specs=pl.BlockSpec((m * SUB, LANES), lambda s, te, nv: (s, 0)),
        scratch_shapes=[pltpu.VMEM((2, m * SUB, LANES), F32), pltpu.SemaphoreType.DMA((2,))],
    )
    return pl.pallas_call(
        _expert_kernel,
        grid_spec=grid_spec,
        out_shape=jax.ShapeDtypeStruct((n_tiles * m * SUB, LANES), F32),
        compiler_params=_cparams(1),
        name="moe_experts",
    )(tile_expert, n_valid, row_token, row_token, hf_tiles, w1, w3, w2)


def _combine_kernel(pos_cur, pos_nxt, x1_ref, mod_ref, ri_ref, gfin_ref, y_hbm, o_ref, ybuf, sem, *, final):
    bi, i = pl.program_id(0), pl.program_id(1)
    nb, ni = pl.num_programs(0), pl.num_programs(1)
    step = bi * ni + i
    slot = step % 2
    rows = ybuf.shape[1] // SUB

    @pl.when(step == 0)
    def _():
        _start_row_gather(pos_cur, rows, y_hbm, ybuf.at[0], sem.at[0])

    @pl.when(step + 1 < nb * ni)
    def _():
        _start_row_gather(pos_nxt, rows, y_hbm, ybuf.at[1 - slot], sem.at[1 - slot])

    _wait_row_gather(rows, y_hbm, ybuf.at[slot], sem.at[slot])
    g_f = mod_ref[0, 0][5:6]
    ri = ri_ref[0]
    p1, p2 = ri[:, 2:3], ri[:, 3:4]
    half = rows // 2
    y = p1 * _from_token_tiles(ybuf.at[slot], 0, half) + p2 * _from_token_tiles(ybuf.at[slot], half, half)
    x2 = x1_ref[0] + g_f * y
    o_ref[0] = _rms(x2, gfin_ref[...]) if final else x2


def _combine_call(pos, x1, mod6, rinfo, gfin, ysorted, n_lat_tiles, final):
    b, n, d = x1.shape
    nt = n // TM
    ni = n_lat_tiles if final else nt

    def pos_tile(step):
        return (step // ni) * nt + step % ni

    def tok(width):
        return pl.BlockSpec((1, TM, width), lambda bi, i: (bi, i, 0))

    kern = functools.partial(_combine_kernel, final=final)
    return pl.pallas_call(
        kern,
        grid=(b, ni),
        in_specs=[pl.BlockSpec((1, 1, 2 * TM), lambda bi, i: (pos_tile(bi * ni + i), 0, 0), memory_space=pltpu.SMEM),
                  pl.BlockSpec((1, 1, 2 * TM),
                               lambda bi, i: (pos_tile(jnp.minimum(bi * ni + i + 1, b * ni - 1)), 0, 0),
                               memory_space=pltpu.SMEM),
                  tok(d),
                  pl.BlockSpec((1, 1, 6, d), lambda bi, i: (bi, jnp.where(i < n_lat_tiles, 1, 0), 0, 0)),
                  tok(LANES),
                  pl.BlockSpec((1, d), lambda bi, i: (0, 0)),
                  pl.BlockSpec(memory_space=pl.ANY)],
        out_specs=tok(d),
        out_shape=jax.ShapeDtypeStruct((b, ni * TM, d), F32),
        scratch_shapes=[pltpu.VMEM((2, 2 * TM * SUB, LANES), F32), pltpu.SemaphoreType.DMA((2,))],
        compiler_params=_cparams(2),
        name="moe_combine_final" if final else "moe_combine",
    )(pos, pos, x1, mod6, rinfo, gfin, ysorted)


def _route(rinfo, b, n):
    t = b * n
    m = TM
    e = rinfo[..., :2].astype(jnp.int32).reshape(t, 2)
    ef = e.T.reshape(-1)
    oh = (ef[:, None] == jnp.arange(N_EXPERTS, dtype=jnp.int32)[None, :]).astype(jnp.int32)
    csum = jnp.cumsum(oh, axis=0)
    rank = jnp.take_along_axis(csum, ef[:, None], axis=1)[:, 0] - 1
    counts = csum[-1]
    padded = ((counts + m - 1) // m) * m
    ends = jnp.cumsum(padded)
    pos = (ends - padded)[ef] + rank
    n_tiles = (2 * t) // m + N_EXPERTS
    n_valid = (ends[-1] // m).astype(jnp.int32).reshape(1)
    tile_start = jnp.arange(n_tiles, dtype=jnp.int32) * m
    tile_expert = jnp.minimum(jnp.sum((tile_start[:, None] >= ends[None, :]).astype(jnp.int32), axis=1),
                              N_EXPERTS - 1).astype(jnp.int32)
    row_token = jnp.zeros((n_tiles * m,), jnp.int32).at[pos].set(
        jnp.arange(2 * t, dtype=jnp.int32) % t, unique_indices=True, mode="promise_in_bounds")
    pos_tiles = pos.reshape(2, t // m, 1, m).transpose(1, 2, 0, 3).reshape(t // m, 1, 2 * m)
    return tile_expert, n_valid, row_token.reshape(n_tiles, 1, m), pos_tiles


def _prep_w_in(w_in):
    d = w_in.shape[0]
    o = np.cumsum([0, LRU_WIDTH, MLA_KV_RANK, MLA_ROPE, GQA_KV_HEADS * GQA_DIM, GQA_KV_HEADS * GQA_DIM,
                   LRU_WIDTH, MLA_Q_RANK, GQA_HEADS * GQA_DIM, N_BRANCH * d])
    xr, ckv, kr, gk, gv, rg, cq, gq, mg = [w_in[:, int(o[i]):int(o[i + 1])] for i in range(9)]
    z = lambda w: jnp.zeros((d, w), w_in.dtype)
    return jnp.concatenate([xr, rg, mg, gq, cq, ckv, z(ROPE_LANE0), kr, z(LANES - ROPE_LANE0 - MLA_ROPE), gk, gv],
                           axis=1).astype(BF16)


def _prep_wuq(wuq):
    r = wuq.shape[0]
    w = wuq.reshape(r, MLA_HEADS, MLA_NOPE + MLA_ROPE)
    w = jnp.pad(w, ((0, 0), (0, 0), (0, MLA_SLOT - MLA_NOPE - MLA_ROPE)))
    return w.reshape(r, MLA_HEADS * MLA_SLOT).astype(BF16)


def _prep_wukv(wukv):
    r = wukv.shape[0]
    w = wukv.reshape(r, MLA_HEADS, MLA_NOPE + MLA_V)
    k = jnp.pad(w[:, :, :MLA_NOPE], ((0, 0), (0, 0), (0, MLA_SLOT - MLA_NOPE))).reshape(r, MLA_HEADS * MLA_SLOT)
    v = w[:, :, MLA_NOPE:].reshape(r, MLA_HEADS * MLA_V)
    return jnp.concatenate([k, v], axis=1).astype(BF16)


def _place_matrix():
    p = np.zeros((LANES, MLA_HEADS * MLA_SLOT), np.float32)
    for hh in range(MLA_HEADS):
        for r in range(MLA_ROPE):
            p[ROPE_LANE0 + r, hh * MLA_SLOT + ROPE_LANE0 + r] = 1.0
    return jnp.asarray(p, BF16)


def _kv_select_matrices():
    kvw = GQA_KV_HEADS * GQA_DIM
    sk = np.zeros((GQA_KV_HEADS, kvw, LANES), np.float32)
    svt = np.zeros((GQA_KV_HEADS, VT_ROWS, kvw), np.float32)
    for g in range(GQA_KV_HEADS):
        for c in range(GQA_DIM):
            sk[g, g * GQA_DIM + c, c] = 1.0
            svt[g, c, g * GQA_DIM + c] = 1.0
    return jnp.asarray(sk, BF16), jnp.asarray(svt, BF16)


def _block_diag_pairs(w):
    per = LANES // LRU_BLOCK
    nd = w.shape[0]
    w = w.reshape(nd, LRU_BLOCKS // per, per, LRU_BLOCK, LRU_BLOCK)
    eye = jnp.eye(per, dtype=w.dtype)
    bd = jnp.einsum("dgpij,pq->dgpiqj", w, eye)
    return bd.reshape(nd, LRU_BLOCKS // per, LANES, LANES)


def kernel(x, c, ctx, c_ctx, w_mod, b_mod, g_mix, g_ffn, w_in, conv_w, conv_b, lru_wa, lru_ba, lru_wi, lru_bi,
           lru_lambda, mla_gq, mla_wuq, mla_gkv, mla_wukv, gqa_gq, gqa_gk, w_branch, w_out, moe_wg, moe_bg,
           moe_we, moe_be, moe_w1, moe_w3, moe_w2, g_final):
    b, seq, d = x.shape
    n_ctx = ctx.shape[1]
    n = n_ctx + seq
    depth = w_mod.shape[0]
    assert n_ctx == TM and seq % KC == 0 and seq % GRID_W == 0 and d == 1024
    n_lat_tiles = seq // TM

    xx = jnp.concatenate([x, ctx], axis=1)
    tab_m, tab_g = _rope_tables(n_ctx, seq)
    place = _place_matrix()
    selk, selv = _kv_select_matrices()
    mod_rows = 16
    cc = jnp.concatenate([c, c_ctx[None, :], jnp.zeros((mod_rows - b - 1, d), F32)], axis=0)

    out = None
    for l in range(depth):
        mod = _mod_call(cc, w_mod[l], b_mod[l])
        mod6 = jnp.stack([jnp.broadcast_to(mod[b].reshape(1, 6, d), (b, 6, d)), mod[:b].reshape(b, 6, d)], axis=1)

        xr, rg, mg, qm, km, vm, qg, kg, vg = _inproj_call(
            xx, mod6, g_mix[l].reshape(1, d), _prep_w_in(w_in[l]), _prep_wuq(mla_wuq[l]), _prep_wukv(mla_wukv[l]),
            place, mla_gq[l].reshape(1, -1), mla_gkv[l].reshape(1, -1),
            jnp.tile(gqa_gq[l], LANES // GQA_DIM).reshape(1, LANES),
            jnp.tile(gqa_gk[l], LANES // GQA_DIM).reshape(1, LANES), tab_m, tab_g, n_lat_tiles)

        xr_t = jnp.transpose(xr, (1, 0, 2))
        wa_bd, wi_bd = _block_diag_pairs(lru_wa[l]), _block_diag_pairs(lru_wi[l])
        lru_args = lambda dr: (conv_w[l], conv_b[l].reshape(1, -1), wa_bd[dr], wi_bd[dr],
                               lru_ba[l, dr].reshape(1, -1), lru_bi[l, dr].reshape(1, -1),
                               lru_lambda[l, dr].reshape(1, -1))
        y_f = _lru_call(xr_t, None, *lru_args(0), n_ctx, False)
        y_t = _lru_call(xr_t, y_f, *lru_args(1), n_ctx, True)
        ylru = jnp.transpose(y_t, (1, 0, 2))

        ym_lat = _mla_attn_call(qm, km, vm, seq, n_ctx, False)
        ym_ctx = _mla_attn_call(qm, km, vm, seq, n_ctx, True)
        yg_lat = _gqa_attn_call(qg, kg, vg, selk, selv, seq, n_ctx, False)
        yg_ctx = _gqa_attn_call(qg, kg, vg, selk, selv, seq, n_ctx, True)

        wr = jnp.concatenate([moe_wg[l], moe_we[l], jnp.zeros((d, LANES - N_GROUPS - N_EXPERTS), F32)], axis=1)
        wr_hi = wr.astype(BF16)
        wr = jnp.stack([wr_hi, (wr - wr_hi.astype(F32)).astype(BF16)])
        br = jnp.concatenate([moe_bg[l], moe_be[l], jnp.zeros((LANES - N_GROUPS - N_EXPERTS,), F32)]).reshape(1, LANES)
        x1, hf, rinfo = _merge_call(xx, mod6, ylru, rg, ym_lat, ym_ctx, yg_lat, yg_ctx, mg,
                                    w_branch[l].astype(BF16), w_out[l].astype(BF16),
                                    g_ffn[l].reshape(1, d), wr, br, n_lat_tiles)

        tile_expert, n_valid, row_token, pos_tiles = _route(rinfo, b, n)
        ysorted = _expert_call(tile_expert, n_valid, row_token, hf,
                               moe_w1[l].astype(BF16), moe_w3[l].astype(BF16), moe_w2[l].astype(BF16))
        final = l == depth - 1
        out = _combine_call(pos_tiles, x1, mod6, rinfo, g_final.reshape(1, d), ysorted, n_lat_tiles, final)
        xx = out
    return out
```

```python
import functools

import numpy as np
import jax
import jax.numpy as jnp
from jax import lax
from jax.experimental import pallas as pl
from jax.experimental.pallas import tpu as pltpu

F32 = jnp.float32
BF16 = jnp.bfloat16

EPS = 1e-6
ROPE_THETA = 10000.0
GRID_W = 64
LOG2E = 1.4426950408889634

LRU_WIDTH = 512
LRU_BLOCKS = 8
LRU_BLOCK = LRU_WIDTH // LRU_BLOCKS
CONV_W = 4
LRU_C = 8.0
MLA_HEADS = 8
MLA_Q_RANK = 256
MLA_KV_RANK = 128
MLA_NOPE = 64
MLA_ROPE = 32
MLA_V = 64
MLA_SCALE = (MLA_NOPE + MLA_ROPE) ** -0.5
GQA_HEADS = 8
GQA_KV_HEADS = 2
GQA_GROUP = GQA_HEADS // GQA_KV_HEADS
GQA_DIM = 64
GQA_SCALE = GQA_DIM ** -0.5
N_BRANCH = 3
BRANCH_W = 512
N_GROUPS = 4
EXPERTS_PER_GROUP = 8
N_EXPERTS = N_GROUPS * EXPERTS_PER_GROUP
D_EXPERT = 256

LANES = 128
TM = 256
KC = 1024
MLA_RQ = 512
MLA_HP = 4
GQA_RQ = 512
VMEM_LIMIT = 56 * 1024 * 1024

C_XR = 0
C_RG = C_XR + LRU_WIDTH
C_MG = C_RG + LRU_WIDTH
C_GQ = C_MG + N_BRANCH * 1024
C_SMALL = C_GQ + GQA_HEADS * GQA_DIM
S_CQ = 0
S_CKV = S_CQ + MLA_Q_RANK
S_KR = S_CKV + MLA_KV_RANK
S_GK = S_KR + LANES
S_GV = S_GK + GQA_KV_HEADS * GQA_DIM
SMALL_W = S_GV + GQA_KV_HEADS * GQA_DIM
MLA_SLOT = 128
ROPE_LANE0 = MLA_NOPE
ONES_ROW = 64
VT_ROWS = 80
assert MLA_V == ONES_ROW and GQA_DIM == ONES_ROW


def _cparams(n_axes):
    return pltpu.CompilerParams(dimension_semantics=("arbitrary",) * n_axes,
                                vmem_limit_bytes=VMEM_LIMIT)


def _rms(x, g):
    return x * lax.rsqrt(jnp.mean(x * x, axis=-1, keepdims=True) + EPS) * g


def _sigmoid(x):
    return 1.0 / (1.0 + jnp.exp(-x))


def _lane_iota(shape):
    return lax.broadcasted_iota(jnp.int32, shape, len(shape) - 1)


def _mod_kernel(c_ref, w_ref, b_ref, o_ref):
    c = c_ref[...]
    s = c * _sigmoid(c)
    o_ref[...] = jnp.dot(s, w_ref[...], preferred_element_type=F32) + b_ref[...]


def _mod_call(cc, w, b):
    rows, d = cc.shape
    n = w.shape[1]
    bn = 512
    return pl.pallas_call(
        _mod_kernel,
        grid=(n // bn,),
        in_specs=[pl.BlockSpec((rows, d), lambda j: (0, 0)),
                  pl.BlockSpec((d, bn), lambda j: (0, j)),
                  pl.BlockSpec((1, bn), lambda j: (0, j))],
        out_specs=pl.BlockSpec((rows, bn), lambda j: (0, j)),
        out_shape=jax.ShapeDtypeStruct((rows, n), F32),
        compiler_params=_cparams(1),
        name="adaln_mod",
    )(cc, w, b.reshape(1, n))


def _rope_tables(n_ctx, seq):
    rows = seq // GRID_W
    row = np.repeat(np.arange(rows), GRID_W).astype(np.float64)
    col = np.tile(np.arange(GRID_W), rows).astype(np.float64)

    def angles(rot_dim):
        quarter = rot_dim // 4
        freqs = ROPE_THETA ** (-np.arange(quarter, dtype=np.float64) / quarter)
        return np.concatenate([row[:, None] * freqs, col[:, None] * freqs], axis=-1)

    n = n_ctx + seq
    am = angles(MLA_ROPE)
    half = MLA_ROPE // 2
    cm = np.ones((n, LANES)); sm = np.zeros((n, LANES))
    cm[:seq, ROPE_LANE0:ROPE_LANE0 + half] = np.cos(am)
    cm[:seq, ROPE_LANE0 + half:ROPE_LANE0 + 2 * half] = np.cos(am)
    sm[:seq, ROPE_LANE0:ROPE_LANE0 + half] = -np.sin(am)
    sm[:seq, ROPE_LANE0 + half:ROPE_LANE0 + 2 * half] = np.sin(am)
    ag = angles(GQA_DIM)
    hg = GQA_DIM // 2
    cg = np.ones((n, LANES)); sg = np.zeros((n, LANES))
    for h0 in (0, GQA_DIM):
        cg[:seq, h0:h0 + hg] = np.cos(ag)
        cg[:seq, h0 + hg:h0 + 2 * hg] = np.cos(ag)
        sg[:seq, h0:h0 + hg] = -np.sin(ag)
        sg[:seq, h0 + hg:h0 + 2 * hg] = np.sin(ag)
    tab_m = np.concatenate([cm, sm], axis=-1).astype(np.float32)
    tab_g = np.concatenate([cg, sg], axis=-1).astype(np.float32)
    return jnp.asarray(tab_m), jnp.asarray(tab_g)


def _rope_mla(y, cos, sin):
    half = MLA_ROPE // 2
    lane = _lane_iota(y.shape)
    rot = jnp.where(lane < ROPE_LANE0 + half,
                    pltpu.roll(y, LANES - half, 1),
                    pltpu.roll(y, half, 1))
    return y * cos + rot * sin


def _rope_gqa(y, cos, sin):
    half = GQA_DIM // 2
    lane = _lane_iota(y.shape)
    rot = jnp.where((lane % GQA_DIM) < half,
                    pltpu.roll(y, LANES - half, 1),
                    pltpu.roll(y, half, 1))
    return y * cos + rot * sin


def _head_rms_pair(t, g):
    lane = _lane_iota(t.shape)
    lo = lane < GQA_DIM
    t2 = t * t
    s_lo = jnp.sum(jnp.where(lo, t2, 0.0), axis=-1, keepdims=True)
    s_hi = jnp.sum(jnp.where(lo, 0.0, t2), axis=-1, keepdims=True)
    r = jnp.where(lo, lax.rsqrt(s_lo * (1.0 / GQA_DIM) + EPS), lax.rsqrt(s_hi * (1.0 / GQA_DIM) + EPS))
    return t * r * g


def _inproj_kernel(x_ref, mod_ref, gmix_ref, w_ref, wuq_ref, wukv_ref, place_ref,
                   gq_ref, gkv_ref, ggq_ref, ggk_ref, tabm_ref, tabg_ref,
                   xr_ref, rg_ref, mg_ref, qm_ref, km_ref, vm_ref, qg_ref, kg_ref, vg_ref):
    x = x_ref[0]
    m6 = mod_ref[0, 0]
    sh, sc = m6[0:1], m6[1:2]
    h = (_rms(x, gmix_ref[...]) * (1.0 + sc) + sh).astype(BF16)

    def proj(c0, width):
        return jnp.dot(h, w_ref[:, c0:c0 + width], preferred_element_type=F32)

    cos_m, sin_m = tabm_ref[:, :LANES], tabm_ref[:, LANES:]
    cos_g, sin_g = tabg_ref[:, :LANES], tabg_ref[:, LANES:]
    lane = _lane_iota((TM, LANES))

    small = proj(C_SMALL, SMALL_W)
    gq = proj(C_GQ, GQA_HEADS * GQA_DIM)
    cq = small[:, S_CQ:S_CQ + MLA_Q_RANK]
    ckv = small[:, S_CKV:S_CKV + MLA_KV_RANK]
    krp = small[:, S_KR:S_KR + LANES]
    gk = small[:, S_GK:S_GK + LANES]
    gv = small[:, S_GV:S_GV + LANES]
    xr_ref[0] = proj(C_XR, LRU_WIDTH)
    rg_ref[0] = proj(C_RG, LRU_WIDTH).astype(BF16)

    cqn = _rms(cq, gq_ref[...]).astype(BF16)
    ckvn = _rms(ckv, gkv_ref[...]).astype(BF16)
    kr = _rope_mla(krp, cos_m, sin_m).astype(BF16)
    qu = jnp.dot(cqn, wuq_ref[...], preferred_element_type=F32)
    kvu = jnp.dot(ckvn, wukv_ref[...], preferred_element_type=F32)
    kr_placed = jnp.dot(kr, place_ref[...], preferred_element_type=F32)

    for j in range(N_BRANCH * 1024 // 512):
        mg_ref[0, :, j * 512:(j + 1) * 512] = proj(C_MG + j * 512, 512).astype(BF16)

    for v in range(GQA_HEADS * GQA_DIM // LANES):
        t = gq[:, v * LANES:(v + 1) * LANES]
        y = _rope_gqa(_head_rms_pair(t, ggq_ref[...]), cos_g, sin_g) * (GQA_SCALE * LOG2E)
        ysw = pltpu.roll(y, GQA_DIM, 1)
        qg_ref[0, :, (2 * v) * LANES:(2 * v + 1) * LANES] = jnp.where(lane < GQA_DIM, y, ysw).astype(BF16)
        qg_ref[0, :, (2 * v + 1) * LANES:(2 * v + 2) * LANES] = jnp.where(lane < GQA_DIM, ysw, y).astype(BF16)
    kg_ref[0] = _rope_gqa(_head_rms_pair(gk, ggk_ref[...]), cos_g, sin_g).astype(BF16)
    vg_ref[0] = gv.astype(BF16)

    for hh in range(MLA_HEADS):
        y = _rope_mla(qu[:, hh * MLA_SLOT:(hh + 1) * MLA_SLOT], cos_m, sin_m)
        qm_ref[0, :, hh * MLA_SLOT:(hh + 1) * MLA_SLOT] = (y * (MLA_SCALE * LOG2E)).astype(BF16)
    n_k = MLA_HEADS * MLA_SLOT
    km_ref[0] = (kvu[:, :n_k] + kr_placed).astype(BF16)
    vm_ref[0] = kvu[:, n_k:].astype(BF16)


def _inproj_call(xx, mod6, gmix, w_all, wuq, wukv, place, gq, gkv, ggq, ggk, tab_m, tab_g, n_lat_tiles):
    b, n, d = xx.shape
    nt = n // TM
    ncol = w_all.shape[1]

    def tok(width):
        return pl.BlockSpec((1, TM, width), lambda bi, i: (bi, i, 0))

    def const(shape):
        return pl.BlockSpec(shape, lambda bi, i: (0,) * len(shape))

    out_w = [(LRU_WIDTH, F32), (LRU_WIDTH, BF16), (N_BRANCH * 1024, BF16),
             (MLA_HEADS * MLA_SLOT, BF16), (MLA_HEADS * MLA_SLOT, BF16), (MLA_HEADS * MLA_V, BF16),
             (GQA_HEADS * LANES, BF16), (GQA_KV_HEADS * GQA_DIM, BF16), (GQA_KV_HEADS * GQA_DIM, BF16)]
    return pl.pallas_call(
        _inproj_kernel,
        grid=(b, nt),
        in_specs=[tok(d),
                  pl.BlockSpec((1, 1, 6, d), lambda bi, i: (bi, jnp.where(i < n_lat_tiles, 1, 0), 0, 0)),
                  const((1, d)), const((d, ncol)), const(wuq.shape), const(wukv.shape), const(place.shape),
                  const((1, MLA_Q_RANK)), const((1, MLA_KV_RANK)), const((1, LANES)), const((1, LANES)),
                  pl.BlockSpec((TM, 2 * LANES), lambda bi, i: (i, 0)),
                  pl.BlockSpec((TM, 2 * LANES), lambda bi, i: (i, 0))],
        out_specs=[tok(w) for w, _ in out_w],
        out_shape=[jax.ShapeDtypeStruct((b, n, w), dt) for w, dt in out_w],
        compiler_params=_cparams(2),
        name="in_proj",
    )(xx, mod6, gmix, w_all, wuq, wukv, place, gq, gkv, ggq, ggk, tab_m, tab_g)


def _flash_t(streams, n_main, tail_start, tail_size):
    nt_dims = (((1,), (1,)), ((), ()))
    ppc = KC // TM

    def scores(start, size):
        return tuple(lax.dot_general(k_fn(start, size), q, nt_dims, preferred_element_type=F32)
                     for q, k_fn, _ in streams)

    def softmax_pv(sts, size, piece0, carries):
        stats = []
        for st, (m, _) in zip(sts, carries):
            m_new = jnp.maximum(m, jnp.max(st, axis=0, keepdims=True))
            stats.append((m_new, jnp.exp2(m - m_new), jnp.exp2(st - m_new).astype(BF16)))
        out = []
        for (_, _, vt_fn), (m_new, alpha, pb), (_, acc) in zip(streams, stats, carries):
            pv = None
            for t in range(size // TM):
                part = jnp.dot(vt_fn(piece0 + t), pb[t * TM:(t + 1) * TM], preferred_element_type=F32)
                pv = part if pv is None else pv + part
            out.append((m_new, alpha * acc + pv))
        return tuple(out)

    carries = tuple((jnp.full((1, q.shape[0]), -jnp.inf, F32), jnp.zeros((VT_ROWS, q.shape[0]), F32))
                    for q, _, _ in streams)
    chunks = [(j * KC, KC) for j in range(n_main)] + [(tail_start, tail_size)]
    sts = scores(*chunks[0])
    for idx, (start, size) in enumerate(chunks):
        nxt = scores(*chunks[idx + 1]) if idx + 1 < len(chunks) else None
        carries = softmax_pv(sts, size, start // TM, carries)
        sts = nxt
    return [acc[:ONES_ROW] * (1.0 / acc[ONES_ROW:ONES_ROW + 1]) for _, acc in carries]


def _values_t(sel, v, row):
    vt = lax.dot_general(sel, v, (((1,), (1,)), ((), ())), preferred_element_type=F32)
    return jnp.where(row == ONES_ROW, 1.0, vt).astype(BF16)


def _mla_attn_kernel(q_ref, k_ref, v_ref, o_ref, vt, *, hp, n_main, tail_start, tail_size):
    nk = k_ref.shape[1]

    @pl.when(pl.program_id(2) == 0)
    def _():
        row = lax.broadcasted_iota(jnp.int32, (VT_ROWS, TM), 0)
        r_i = lax.broadcasted_iota(jnp.int32, (VT_ROWS, LANES), 0)
        c_i = _lane_iota((VT_ROWS, LANES))
        for hh in range(hp):
            sel = jnp.where(jnp.logical_and(r_i < MLA_V, c_i == r_i + (hh % 2) * MLA_V), 1.0, 0.0).astype(BF16)
            for c in range(nk // TM):
                v = v_ref[0, c * TM:(c + 1) * TM, (hh // 2) * LANES:(hh // 2 + 1) * LANES]
                vt[hh, c] = _values_t(sel, v, row)

    streams = []
    for hh in range(hp):
        q = q_ref[0, :, hh * MLA_SLOT:(hh + 1) * MLA_SLOT]
        k_fn = functools.partial(lambda st, sz, c0: k_ref[0, pl.ds(st, sz), c0:c0 + MLA_SLOT], c0=hh * MLA_SLOT)
        vt_fn = functools.partial(lambda piece, h: vt[h, piece], h=hh)
        streams.append((q, k_fn, vt_fn))
    outs = _flash_t(streams, n_main, tail_start, tail_size)
    for pr in range(hp // 2):
        o_t = jnp.concatenate([outs[2 * pr], outs[2 * pr + 1]], axis=0)
        o_ref[0, :, pr * LANES:(pr + 1) * LANES] = o_t.T.astype(BF16)


def _mla_attn_call(qm, km, vm, seq, n_ctx, ctx_queries):
    b = qm.shape[0]
    hp = MLA_HP
    if ctx_queries:
        rq, nq, q0, nk, k0 = n_ctx, 1, seq // n_ctx, n_ctx, seq // n_ctx
        plan = dict(n_main=0, tail_start=0, tail_size=n_ctx)
    else:
        rq, nq, q0, nk, k0 = MLA_RQ, seq // MLA_RQ, 0, seq + n_ctx, 0
        plan = dict(n_main=seq // KC - 1, tail_start=seq - KC, tail_size=KC + n_ctx)
    kern = functools.partial(_mla_attn_kernel, hp=hp, **plan)
    return pl.pallas_call(
        kern,
        grid=(b, MLA_HEADS // hp, nq),
        in_specs=[pl.BlockSpec((1, rq, hp * MLA_SLOT), lambda bi, p, i: (bi, i + q0, p)),
                  pl.BlockSpec((1, nk, hp * MLA_SLOT), lambda bi, p, i: (bi, k0, p)),
                  pl.BlockSpec((1, nk, hp * MLA_V), lambda bi, p, i: (bi, k0, p))],
        out_specs=pl.BlockSpec((1, rq, hp * MLA_V), lambda bi, p, i: (bi, i, p)),
        out_shape=jax.ShapeDtypeStruct((b, rq * nq, MLA_HEADS * MLA_V), BF16),
        scratch_shapes=[pltpu.VMEM((hp, nk // TM, VT_ROWS, TM), BF16)],
        compiler_params=_cparams(3),
        name="mla_attention_ctx" if ctx_queries else "mla_attention",
    )(qm, km, vm)


def _gqa_attn_kernel(q_ref, k_ref, v_ref, selk_ref, selvt_ref, o_ref, kd, vt, *, n_main, tail_start, tail_size):
    i = pl.program_id(2)
    nk = kd.shape[0]
    rq = q_ref.shape[1]

    @pl.when(i == 0)
    def _():
        row = lax.broadcasted_iota(jnp.int32, (VT_ROWS, TM), 0)
        for c in range(nk // TM):
            rows = pl.ds(c * TM, TM)
            kd[rows, :] = jnp.dot(k_ref[0, rows, :], selk_ref[0], preferred_element_type=F32).astype(BF16)
            vt[c] = _values_t(selvt_ref[0], v_ref[0, rows, :], row)

    streams = [(q_ref[0, :, j * LANES:(j + 1) * LANES], lambda st, sz: kd[pl.ds(st, sz), :], lambda piece: vt[piece])
               for j in range(GQA_GROUP)]
    outs = _flash_t(streams, n_main, tail_start, tail_size)
    for pr in range(GQA_GROUP // 2):
        o_t = jnp.concatenate([outs[2 * pr], outs[2 * pr + 1]], axis=0)
        o_ref[0, :, pr * LANES:(pr + 1) * LANES] = o_t.T.astype(BF16)


def _gqa_attn_call(qg, kg, vg, selk, selv, seq, n_ctx, ctx_queries):
    b = qg.shape[0]
    gw = GQA_GROUP * GQA_DIM
    kvw = GQA_KV_HEADS * GQA_DIM
    if ctx_queries:
        rq, nq, q0, nk, k0 = n_ctx, 1, seq // n_ctx, n_ctx, seq // n_ctx
        plan = dict(n_main=0, tail_start=0, tail_size=n_ctx)
    else:
        rq, nq, q0, nk, k0 = GQA_RQ, seq // GQA_RQ, 0, seq + n_ctx, 0
        plan = dict(n_main=seq // KC - 1, tail_start=seq - KC, tail_size=KC + n_ctx)
    kern = functools.partial(_gqa_attn_kernel, **plan)
    return pl.pallas_call(
        kern,
        grid=(b, GQA_KV_HEADS, nq),
        in_specs=[pl.BlockSpec((1, rq, GQA_GROUP * LANES), lambda bi, g, i: (bi, i + q0, g)),
                  pl.BlockSpec((1, nk, kvw), lambda bi, g, i: (bi, k0, 0)),
                  pl.BlockSpec((1, nk, kvw), lambda bi, g, i: (bi, k0, 0)),
                  pl.BlockSpec((1, kvw, LANES), lambda bi, g, i: (g, 0, 0)),
                  pl.BlockSpec((1, VT_ROWS, kvw), lambda bi, g, i: (g, 0, 0))],
        out_specs=pl.BlockSpec((1, rq, gw), lambda bi, g, i: (bi, i, g)),
        out_shape=jax.ShapeDtypeStruct((b, rq * nq, GQA_HEADS * GQA_DIM), BF16),
        scratch_shapes=[pltpu.VMEM((nk, LANES), BF16), pltpu.VMEM((nk // TM, VT_ROWS, TM), BF16)],
        compiler_params=_cparams(3),
        name="gqa_attention_ctx" if ctx_queries else "gqa_attention",
    )(qg, kg, vg, selk, selv)


def _softplus(z):
    return jnp.maximum(z, 0.0) + jnp.log1p(jnp.exp(-jnp.abs(z)))


def _lru_chunk(j, reverse, n_lat_chunks):
    return jnp.where(j == 0, n_lat_chunks, n_lat_chunks - j if reverse else j - 1)


def _lru_kernel(*refs, reverse, n_chunks, n_lat_chunks):
    if reverse:
        (x_ref, hp_ref, hn_ref, cw_ref, cb_ref, wa_ref, wi_ref, ba_ref, bi_ref, lam_ref, yin_ref,
         y_ref, xs, a_s, b_s, h_s) = refs
    else:
        (x_ref, hp_ref, hn_ref, cw_ref, cb_ref, wa_ref, wi_ref, ba_ref, bi_ref, lam_ref,
         y_ref, xs, a_s, b_s, h_s) = refs
        yin_ref = None
    j = pl.program_id(1)
    c = _lru_chunk(j, reverse, n_lat_chunks)
    ch, bsz, lw = x_ref.shape

    @pl.when(j == 0)
    def _():
        h_s[...] = jnp.zeros_like(h_s)

    seq_first = jnp.logical_or(c == 0, c == n_lat_chunks)
    seq_last = jnp.logical_or(c == n_lat_chunks - 1, c == n_chunks - 1)
    xs[0:2] = jnp.where(seq_first, 0.0, hp_ref[...])
    xs[2:2 + ch] = x_ref[...]
    xs[2 + ch:3 + ch] = jnp.where(seq_last, 0.0, hn_ref[...])
    xc = cb_ref[...]
    for tap in range(CONV_W):
        xc = xc + cw_ref[tap:tap + 1, :] * xs[tap:tap + ch]
    x2 = xc.reshape(ch * bsz, lw)
    r = _sigmoid(jnp.dot(x2, wa_ref[0], preferred_element_type=F32) + ba_ref[...])
    gi = _sigmoid(jnp.dot(x2, wi_ref[0], preferred_element_type=F32) + bi_ref[...])
    log_a = (-LRU_C) * r * _softplus(-lam_ref[...])
    a = jnp.exp(log_a)
    a_s[...] = a.reshape(ch, bsz, lw)
    mult = jnp.sqrt(-jnp.tanh(log_a) * (a * a + 1.0))
    b_s[...] = (mult * gi * x2).reshape(ch, bsz, lw)

    def body(tt, h):
        t = ch - 1 - tt if reverse else tt
        h = a_s[t] * h + b_s[t]
        y_ref[t] = h + yin_ref[t] if reverse else h
        return h

    h_s[...] = lax.fori_loop(0, ch, body, h_s[...], unroll=8)


def _lru_call(xr_t, yin, cw, cb, wa_bd, wi_bd, ba, bi, lam, n_ctx, reverse):
    n, bsz, width = xr_t.shape
    ch = TM
    n_chunks = n // ch
    n_lat_chunks = (n - n_ctx) // ch
    groups = width // LANES
    cmap = functools.partial(_lru_chunk, reverse=reverse, n_lat_chunks=n_lat_chunks)

    main = pl.BlockSpec((ch, bsz, LANES), lambda g, j: (cmap(j), 0, g))
    in_specs = [main,
                pl.BlockSpec((2, bsz, LANES), lambda g, j: (jnp.maximum(cmap(j) * (ch // 2) - 1, 0), 0, g)),
                pl.BlockSpec((1, bsz, LANES), lambda g, j: (jnp.minimum((cmap(j) + 1) * ch, n - 1), 0, g)),
                pl.BlockSpec((CONV_W, LANES), lambda g, j: (0, g)),
                pl.BlockSpec((1, LANES), lambda g, j: (0, g)),
                pl.BlockSpec((1, LANES, LANES), lambda g, j: (g, 0, 0)),
                pl.BlockSpec((1, LANES, LANES), lambda g, j: (g, 0, 0)),
                pl.BlockSpec((1, LANES), lambda g, j: (0, g)),
                pl.BlockSpec((1, LANES), lambda g, j: (0, g)),
                pl.BlockSpec((1, LANES), lambda g, j: (0, g))]
    args = [xr_t, xr_t, xr_t, cw, cb, wa_bd, wi_bd, ba, bi, lam]
    if reverse:
        in_specs.append(main)
        args.append(yin)
    kern = functools.partial(_lru_kernel, reverse=reverse, n_chunks=n_chunks, n_lat_chunks=n_lat_chunks)
    return pl.pallas_call(
        kern,
        grid=(groups, n_chunks),
        in_specs=in_specs,
        out_specs=main,
        out_shape=jax.ShapeDtypeStruct((n, bsz, width), F32),
        scratch_shapes=[pltpu.VMEM((ch + 3, bsz, LANES), F32), pltpu.VMEM((ch, bsz, LANES), F32),
                        pltpu.VMEM((ch, bsz, LANES), F32), pltpu.VMEM((bsz, LANES), F32)],
        compiler_params=_cparams(2),
        name="rglru_rev" if reverse else "rglru_fwd",
    )(*args)


def _gelu_tanh(x):
    return 0.5 * x * (1.0 + jnp.tanh(0.7978845608028654 * (x + 0.044715 * (x * x * x))))


def _merge_kernel(x_ref, mod_ref, ylru_ref, rg_ref, yml_ref, ymc_ref, ygl_ref, ygc_ref, mg_ref, wbr_ref, wout_ref,
                  gffn_ref, wr_ref, br_ref, x1_ref, hf_ref, ri_ref, *, n_lat_tiles):
    d = x_ref.shape[-1]
    is_lat = pl.program_id(1) < n_lat_tiles
    m6 = mod_ref[0, 0]
    g_a, sh_f, sc_f = m6[2:3], m6[3:4], m6[4:5]
    y_rnn = (ylru_ref[0] * _gelu_tanh(rg_ref[0].astype(F32))).astype(BF16)
    y_mla = jnp.where(is_lat, yml_ref[0], ymc_ref[0])
    y_gqa = jnp.where(is_lat, ygl_ref[0], ygc_ref[0])
    acc = jnp.zeros((TM, d), F32)
    for kbr, br in enumerate((y_rnn, y_mla, y_gqa)):
        pr = jnp.dot(br, wbr_ref[kbr], preferred_element_type=F32)
        acc = acc + _sigmoid(mg_ref[0, :, kbr * d:(kbr + 1) * d].astype(F32)) * pr
    out = jnp.dot(acc.astype(BF16), wout_ref[...], preferred_element_type=F32)
    x1 = x_ref[0] + g_a * out
    x1_ref[0] = x1
    hf = _rms(x1, gffn_ref[...]) * (1.0 + sc_f) + sh_f
    _to_token_tiles(hf_ref, hf)
    hf_hi = hf.astype(BF16)
    hf_lo = (hf - hf_hi.astype(F32)).astype(BF16)
    lg = (jnp.dot(hf_hi, wr_ref[0], preferred_element_type=F32)
          + jnp.dot(hf_lo, wr_ref[0], preferred_element_type=F32)
          + jnp.dot(hf_hi, wr_ref[1], preferred_element_type=F32)) + br_ref[...]
    ri_ref[0] = _route_top2(lg)


def _route_top2(lg):
    lane = _lane_iota(lg.shape)
    big = jnp.int32(1 << 20)
    neg = -jnp.inf
    is_g = lane < N_GROUPS
    gl = jnp.where(is_g, lg, neg)
    gmax = jnp.max(gl, axis=-1, keepdims=True)
    gsel = jnp.min(jnp.where(gl == gmax, lane, big), axis=-1, keepdims=True)
    pg = 1.0 / jnp.sum(jnp.where(is_g, jnp.exp(lg - gmax), 0.0), axis=-1, keepdims=True)
    e0 = N_GROUPS + gsel * EXPERTS_PER_GROUP
    el = jnp.where(jnp.logical_and(lane >= e0, lane < e0 + EXPERTS_PER_GROUP), lg, neg)
    v1 = jnp.max(el, axis=-1, keepdims=True)
    i1 = jnp.min(jnp.where(el == v1, lane, big), axis=-1, keepdims=True)
    el2 = jnp.where(lane == i1, neg, el)
    v2 = jnp.max(el2, axis=-1, keepdims=True)
    i2 = jnp.min(jnp.where(el2 == v2, lane, big), axis=-1, keepdims=True)
    t = jnp.exp(v2 - v1)
    p1 = pg / (1.0 + t)
    p2 = p1 * t
    return jnp.where(lane == 0, (i1 - N_GROUPS).astype(F32),
                     jnp.where(lane == 1, (i2 - N_GROUPS).astype(F32),
                               jnp.where(lane == 2, p1, jnp.where(lane == 3, p2, 0.0))))


def _merge_call(xx, mod6, ylru, rg, ym_lat, ym_ctx, yg_lat, yg_ctx, mg, wbr, wout, gffn, wr, br, n_lat_tiles):
    b, n, d = xx.shape
    nt = n // TM

    def tok(width):
        return pl.BlockSpec((1, TM, width), lambda bi, i: (bi, i, 0))

    def lat(width):
        return pl.BlockSpec((1, TM, width), lambda bi, i: (bi, jnp.minimum(i, n_lat_tiles - 1), 0))

    def ctx(width):
        return pl.BlockSpec((1, TM, width), lambda bi, i: (bi, jnp.maximum(i - n_lat_tiles, 0), 0))

    def const(shape):
        return pl.BlockSpec(shape, lambda bi, i: (0,) * len(shape))

    return pl.pallas_call(
        functools.partial(_merge_kernel, n_lat_tiles=n_lat_tiles),
        grid=(b, nt),
        in_specs=[tok(d),
                  pl.BlockSpec((1, 1, 6, d), lambda bi, i: (bi, jnp.where(i < n_lat_tiles, 1, 0), 0, 0)),
                  tok(BRANCH_W), tok(BRANCH_W), lat(BRANCH_W), ctx(BRANCH_W), lat(BRANCH_W), ctx(BRANCH_W),
                  tok(N_BRANCH * d),
                  const(wbr.shape), const(wout.shape), const((1, d)), const(wr.shape), const((1, LANES))],
        out_specs=[tok(d), pl.BlockSpec((TM * SUB, LANES), lambda bi, i: (bi * nt + i, 0)), tok(LANES)],
        out_shape=[jax.ShapeDtypeStruct((b, n, d), F32), jax.ShapeDtypeStruct((b * n * SUB, LANES), F32),
                   jax.ShapeDtypeStruct((b, n, LANES), F32)],
        compiler_params=_cparams(2),
        name="merge_router",
    )(xx, mod6, ylru, rg, ym_lat, ym_ctx, yg_lat, yg_ctx, mg, wbr, wout, gffn, wr, br)


SUB = 8


def _to_token_tiles(ref, x, row0=0):
    rows = x.shape[0]
    for j in range(SUB):
        ref[pl.ds(row0 * SUB + j, rows, stride=SUB), :] = x[:, j * LANES:(j + 1) * LANES]


def _from_token_tiles(ref, row0, rows):
    return jnp.concatenate([ref[pl.ds(row0 * SUB + j, rows, stride=SUB), :] for j in range(SUB)], axis=1)


def _start_row_gather(idx_ref, n_rows, src_hbm, dst, sem):
    def body(rb, carry):
        for u in range(SUB):
            tok = idx_ref[0, 0, rb * SUB + u]
            pltpu.make_async_copy(src_hbm.at[pl.ds(pl.multiple_of(tok * SUB, SUB), SUB)],
                                  dst.at[pl.ds(pl.multiple_of(rb * (SUB * SUB) + u * SUB, SUB), SUB)], sem
                                  ).start(priority=u % 2)
        return carry
    lax.fori_loop(0, n_rows // SUB, body, 0)


def _wait_row_gather(n_rows, src_hbm, dst, sem):
    for r in range(n_rows):
        pltpu.make_async_copy(src_hbm.at[pl.ds(0, SUB)], dst.at[pl.ds(r * SUB, SUB)], sem).wait()


def _expert_kernel(te_ref, nv_ref, rt_cur, rt_nxt, hf_hbm, w1_ref, w3_ref, w2_ref, y_ref, xbuf, sem):
    s = pl.program_id(0)
    nv = nv_ref[0]
    slot = s % 2
    m = xbuf.shape[1] // SUB

    @pl.when(jnp.logical_and(s == 0, nv > 0))
    def _():
        _start_row_gather(rt_cur, m, hf_hbm, xbuf.at[0], sem.at[0])

    @pl.when(s + 1 < nv)
    def _():
        _start_row_gather(rt_nxt, m, hf_hbm, xbuf.at[1 - slot], sem.at[1 - slot])

    @pl.when(s < nv)
    def _():
        _wait_row_gather(m, hf_hbm, xbuf.at[slot], sem.at[slot])
        x = _from_token_tiles(xbuf.at[slot], 0, m).astype(BF16)
        h1 = jnp.dot(x, w1_ref[0], preferred_element_type=F32)
        h3 = jnp.dot(x, w3_ref[0], preferred_element_type=F32)
        a = (h1 * _sigmoid(h1) * h3).astype(BF16)
        _to_token_tiles(y_ref, jnp.dot(a, w2_ref[0], preferred_element_type=F32))

    @pl.when(s >= nv)
    def _():
        y_ref[...] = jnp.zeros_like(y_ref)


def _expert_call(tile_expert, n_valid, row_token, hf_tiles, w1, w3, w2):
    n_tiles = tile_expert.shape[0]
    d, de = w1.shape[1], w1.shape[2]
    assert d == SUB * LANES
    m = TM
    grid_spec = pltpu.PrefetchScalarGridSpec(
        num_scalar_prefetch=2,
        grid=(n_tiles,),
        in_specs=[pl.BlockSpec((1, 1, m), lambda s, te, nv: (s, 0, 0), memory_space=pltpu.SMEM),
                  pl.BlockSpec((1, 1, m), lambda s, te, nv: (jnp.minimum(s + 1, n_tiles - 1), 0, 0),
                               memory_space=pltpu.SMEM),
                  pl.BlockSpec(memory_space=pl.ANY),
                  pl.BlockSpec((1, d, de), lambda s, te, nv: (te[s], 0, 0)),
                  pl.BlockSpec((1, d, de), lambda s, te, nv: (te[s], 0, 0)),
                  pl.BlockSpec((1, de, d), lambda s, te, nv: (te[s], 0, 0))],
        out_specs=pl.BlockSpec((m * SUB, LANES), lambda s, te, nv: (s, 0)),
        scratch_shapes=[pltpu.VMEM((2, m * SUB, LANES), F32), pltpu.SemaphoreType.DMA((2,))],
    )
    return pl.pallas_call(
        _expert_kernel,
        grid_spec=grid_spec,
        out_shape=jax.ShapeDtypeStruct((n_tiles * m * SUB, LANES), F32),
        compiler_params=_cparams(1),
        name="moe_experts",
    )(tile_expert, n_valid, row_token, row_token, hf_tiles, w1, w3, w2)


def _combine_kernel(pos_cur, pos_nxt, x1_ref, mod_ref, ri_ref, gfin_ref, y_hbm, o_ref, ybuf, sem, *, final):
    bi, i = pl.program_id(0), pl.program_id(1)
    nb, ni = pl.num_programs(0), pl.num_programs(1)
    step = bi * ni + i
    slot = step % 2
    rows = ybuf.shape[1] // SUB

    @pl.when(step == 0)
    def _():
        _start_row_gather(pos_cur, rows, y_hbm, ybuf.at[0], sem.at[0])

    @pl.when(step + 1 < nb * ni)
    def _():
        _start_row_gather(pos_nxt, rows, y_hbm, ybuf.at[1 - slot], sem.at[1 - slot])

    _wait_row_gather(rows, y_hbm, ybuf.at[slot], sem.at[slot])
    g_f = mod_ref[0, 0][5:6]
    ri = ri_ref[0]
    p1, p2 = ri[:, 2:3], ri[:, 3:4]
    half = rows // 2
    y = p1 * _from_token_tiles(ybuf.at[slot], 0, half) + p2 * _from_token_tiles(ybuf.at[slot], half, half)
    x2 = x1_ref[0] + g_f * y
    o_ref[0] = _rms(x2, gfin_ref[...]) if final else x2


def _combine_call(pos, x1, mod6, rinfo, gfin, ysorted, n_lat_tiles, final):
    b, n, d = x1.shape
    nt = n // TM
    ni = n_lat_tiles if final else nt

    def pos_tile(step):
        return (step // ni) * nt + step % ni

    def tok(width):
        return pl.BlockSpec((1, TM, width), lambda bi, i: (bi, i, 0))

    kern = functools.partial(_combine_kernel, final=final)
    return pl.pallas_call(
        kern,
        grid=(b, ni),
        in_specs=[pl.BlockSpec((1, 1, 2 * TM), lambda bi, i: (pos_tile(bi * ni + i), 0, 0), memory_space=pltpu.SMEM),
                  pl.BlockSpec((1, 1, 2 * TM),
                               lambda bi, i: (pos_tile(jnp.minimum(bi * ni + i + 1, b * ni - 1)), 0, 0),
                               memory_space=pltpu.SMEM),
                  tok(d),
                  pl.BlockSpec((1, 1, 6, d), lambda bi, i: (bi, jnp.where(i < n_lat_tiles, 1, 0), 0, 0)),
                  tok(LANES),
                  pl.BlockSpec((1, d), lambda bi, i: (0, 0)),
                  pl.BlockSpec(memory_space=pl.ANY)],
        out_specs=tok(d),
        out_shape=jax.ShapeDtypeStruct((b, ni * TM, d), F32),
        scratch_shapes=[pltpu.VMEM((2, 2 * TM * SUB, LANES), F32), pltpu.SemaphoreType.DMA((2,))],
        compiler_params=_cparams(2),
        name="moe_combine_final" if final else "moe_combine",
    )(pos, pos, x1, mod6, rinfo, gfin, ysorted)


def _route(rinfo, b, n):
    t = b * n
    m = TM
    e = rinfo[..., :2].astype(jnp.int32).reshape(t, 2)
    ef = e.T.reshape(-1)
    oh = (ef[:, None] == jnp.arange(N_EXPERTS, dtype=jnp.int32)[None, :]).astype(jnp.int32)
    csum = jnp.cumsum(oh, axis=0)
    rank = jnp.take_along_axis(csum, ef[:, None], axis=1)[:, 0] - 1
    counts = csum[-1]
    padded = ((counts + m - 1) // m) * m
    ends = jnp.cumsum(padded)
    pos = (ends - padded)[ef] + rank
    n_tiles = (2 * t) // m + N_EXPERTS
    n_valid = (ends[-1] // m).astype(jnp.int32).reshape(1)
    tile_start = jnp.arange(n_tiles, dtype=jnp.int32) * m
    tile_expert = jnp.minimum(jnp.sum((tile_start[:, None] >= ends[None, :]).astype(jnp.int32), axis=1),
                              N_EXPERTS - 1).astype(jnp.int32)
    row_token = jnp.zeros((n_tiles * m,), jnp.int32).at[pos].set(
        jnp.arange(2 * t, dtype=jnp.int32) % t, unique_indices=True, mode="promise_in_bounds")
    pos_tiles = pos.reshape(2, t // m, 1, m).transpose(1, 2, 0, 3).reshape(t // m, 1, 2 * m)
    return tile_expert, n_valid, row_token.reshape(n_tiles, 1, m), pos_tiles


def _prep_w_in(w_in):
    d = w_in.shape[0]
    o = np.cumsum([0, LRU_WIDTH, MLA_KV_RANK, MLA_ROPE, GQA_KV_HEADS * GQA_DIM, GQA_KV_HEADS * GQA_DIM,
                   LRU_WIDTH, MLA_Q_RANK, GQA_HEADS * GQA_DIM, N_BRANCH * d])
    xr, ckv, kr, gk, gv, rg, cq, gq, mg = [w_in[:, int(o[i]):int(o[i + 1])] for i in range(9)]
    z = lambda w: jnp.zeros((d, w), w_in.dtype)
    return jnp.concatenate([xr, rg, mg, gq, cq, ckv, z(ROPE_LANE0), kr, z(LANES - ROPE_LANE0 - MLA_ROPE), gk, gv],
                           axis=1).astype(BF16)


def _prep_wuq(wuq):
    r = wuq.shape[0]
    w = wuq.reshape(r, MLA_HEADS, MLA_NOPE + MLA_ROPE)
    w = jnp.pad(w, ((0, 0), (0, 0), (0, MLA_SLOT - MLA_NOPE - MLA_ROPE)))
    return w.reshape(r, MLA_HEADS * MLA_SLOT).astype(BF16)


def _prep_wukv(wukv):
    r = wukv.shape[0]
    w = wukv.reshape(r, MLA_HEADS, MLA_NOPE + MLA_V)
    k = jnp.pad(w[:, :, :MLA_NOPE], ((0, 0), (0, 0), (0, MLA_SLOT - MLA_NOPE))).reshape(r, MLA_HEADS * MLA_SLOT)
    v = w[:, :, MLA_NOPE:].reshape(r, MLA_HEADS * MLA_V)
    return jnp.concatenate([k, v], axis=1).astype(BF16)


def _place_matrix():
    p = np.zeros((LANES, MLA_HEADS * MLA_SLOT), np.float32)
    for hh in range(MLA_HEADS):
        for r in range(MLA_ROPE):
            p[ROPE_LANE0 + r, hh * MLA_SLOT + ROPE_LANE0 + r] = 1.0
    return jnp.asarray(p, BF16)


def _kv_select_matrices():
    kvw = GQA_KV_HEADS * GQA_DIM
    sk = np.zeros((GQA_KV_HEADS, kvw, LANES), np.float32)
    svt = np.zeros((GQA_KV_HEADS, VT_ROWS, kvw), np.float32)
    for g in range(GQA_KV_HEADS):
        for c in range(GQA_DIM):
            sk[g, g * GQA_DIM + c, c] = 1.0
            svt[g, c, g * GQA_DIM + c] = 1.0
    return jnp.asarray(sk, BF16), jnp.asarray(svt, BF16)


def _block_diag_pairs(w):
    per = LANES // LRU_BLOCK
    nd = w.shape[0]
    w = w.reshape(nd, LRU_BLOCKS // per, per, LRU_BLOCK, LRU_BLOCK)
    eye = jnp.eye(per, dtype=w.dtype)
    bd = jnp.einsum("dgpij,pq->dgpiqj", w, eye)
    return bd.reshape(nd, LRU_BLOCKS // per, LANES, LANES)


def kernel(x, c, ctx, c_ctx, w_mod, b_mod, g_mix, g_ffn, w_in, conv_w, conv_b, lru_wa, lru_ba, lru_wi, lru_bi,
           lru_lambda, mla_gq, mla_wuq, mla_gkv, mla_wukv, gqa_gq, gqa_gk, w_branch, w_out, moe_wg, moe_bg,
           moe_we, moe_be, moe_w1, moe_w3, moe_w2, g_final):
    b, seq, d = x.shape
    n_ctx = ctx.shape[1]
    n = n_ctx + seq
    depth = w_mod.shape[0]
    assert n_ctx == TM and seq % KC == 0 and seq % GRID_W == 0 and d == 1024
    n_lat_tiles = seq // TM

    xx = jnp.concatenate([x, ctx], axis=1)
    tab_m, tab_g = _rope_tables(n_ctx, seq)
    place = _place_matrix()
    selk, selv = _kv_select_matrices()
    mod_rows = 16
    cc = jnp.concatenate([c, c_ctx[None, :], jnp.zeros((mod_rows - b - 1, d), F32)], axis=0)

    out = None
    for l in range(depth):
        mod = _mod_call(cc, w_mod[l], b_mod[l])
        mod6 = jnp.stack([jnp.broadcast_to(mod[b].reshape(1, 6, d), (b, 6, d)), mod[:b].reshape(b, 6, d)], axis=1)

        xr, rg, mg, qm, km, vm, qg, kg, vg = _inproj_call(
            xx, mod6, g_mix[l].reshape(1, d), _prep_w_in(w_in[l]), _prep_wuq(mla_wuq[l]), _prep_wukv(mla_wukv[l]),
            place, mla_gq[l].reshape(1, -1), mla_gkv[l].reshape(1, -1),
            jnp.tile(gqa_gq[l], LANES // GQA_DIM).reshape(1, LANES),
            jnp.tile(gqa_gk[l], LANES // GQA_DIM).reshape(1, LANES), tab_m, tab_g, n_lat_tiles)

        xr_t = jnp.transpose(xr, (1, 0, 2))
        wa_bd, wi_bd = _block_diag_pairs(lru_wa[l]), _block_diag_pairs(lru_wi[l])
        lru_args = lambda dr: (conv_w[l], conv_b[l].reshape(1, -1), wa_bd[dr], wi_bd[dr],
                               lru_ba[l, dr].reshape(1, -1), lru_bi[l, dr].reshape(1, -1),
                               lru_lambda[l, dr].reshape(1, -1))
        y_f = _lru_call(xr_t, None, *lru_args(0), n_ctx, False)
        y_t = _lru_call(xr_t, y_f, *lru_args(1), n_ctx, True)
        ylru = jnp.transpose(y_t, (1, 0, 2))

        ym_lat = _mla_attn_call(qm, km, vm, seq, n_ctx, False)
        ym_ctx = _mla_attn_call(qm, km, vm, seq, n_ctx, True)
        yg_lat = _gqa_attn_call(qg, kg, vg, selk, selv, seq, n_ctx, False)
        yg_ctx = _gqa_attn_call(qg, kg, vg, selk, selv, seq, n_ctx, True)

        wr = jnp.concatenate([moe_wg[l], moe_we[l], jnp.zeros((d, LANES - N_GROUPS - N_EXPERTS), F32)], axis=1)
        wr_hi = wr.astype(BF16)
        wr = jnp.stack([wr_hi, (wr - wr_hi.astype(F32)).astype(BF16)])
        br = jnp.concatenate([moe_bg[l], moe_be[l], jnp.zeros((LANES - N_GROUPS - N_EXPERTS,), F32)]).reshape(1, LANES)
        x1, hf, rinfo = _merge_call(xx, mod6, ylru, rg, ym_lat, ym_ctx, yg_lat, yg_ctx, mg,
                                    w_branch[l].astype(BF16), w_out[l].astype(BF16),
                                    g_ffn[l].reshape(1, d), wr, br, n_lat_tiles)

        tile_expert, n_valid, row_token, pos_tiles = _route(rinfo, b, n)
        ysorted = _expert_call(tile_expert, n_valid, row_token, hf,
                               moe_w1[l].astype(BF16), moe_w3[l].astype(BF16), moe_w2[l].astype(BF16))
        final = l == depth - 1
        out = _combine_call(pos_tiles, x1, mod6, rinfo, g_final.reshape(1, d), ysorted, n_lat_tiles, final)
        xx = out
    return out
```
